```python
import math
import jax, jax.numpy as jnp
from jax import lax
import numpy as np

D_MODEL = 1024
BATCH = 8
SEQ = 4096
DEPTH = 1

LRU_WIDTH = 1024
LRU_HEADS = 16
LRU_HEAD_DIM = LRU_WIDTH // LRU_HEADS
LRU_CONV = 4
LRU_C = 8.0
HYENA_WIDTH = 1024
HYENA_ORDER = 2
HYENA_CONV = 3
FILTER_EMB = 33
FILTER_BANDS = (FILTER_EMB - 1) // 2
FILTER_HIDDEN = 64
DECAY_FAST = 0.3
DECAY_SLOW = 1.5
DECAY_TARGET = 1e-2
N_BRANCHES = 2
IN_COLS = 2 * LRU_WIDTH + (HYENA_ORDER + 1) * HYENA_WIDTH + N_BRANCHES * D_MODEL
N_EXPERTS = 16
CAPACITY_FACTOR = 2
D_FF_EXPERT = 2 * D_MODEL
EPS = 1e-6

kernel_name = "hybrid_rglru_hyena_ecmoe_encoder"


def rmsnorm(x, g):
    xf = x.astype(jnp.float32)
    y = xf * lax.rsqrt(jnp.mean(xf * xf, axis=-1, keepdims=True) + EPS)
    return (y * g.astype(jnp.float32)).astype(x.dtype)


def dwconv(x, w, b, pad):
    C = x.shape[-1]
    y = lax.conv_general_dilated(x, w[:, None, :].astype(x.dtype), window_strides=(1,), padding=[pad],
                                 dimension_numbers=("NWC", "WIO", "NWC"), feature_group_count=C)
    return y + b.astype(x.dtype)


def rg_lru(xa, w_r, b_r, w_i, b_i, lam, reverse):
    B, S, W = xa.shape
    xh = xa.reshape(B, S, LRU_HEADS, LRU_HEAD_DIM)
    r = jax.nn.sigmoid(jnp.einsum('bshi,hij->bshj', xh, w_r.astype(jnp.float32)).reshape(B, S, W) + b_r.astype(jnp.float32))
    i = jax.nn.sigmoid(jnp.einsum('bshi,hij->bshj', xh, w_i.astype(jnp.float32)).reshape(B, S, W) + b_i.astype(jnp.float32))
    log_a = -LRU_C * r * jax.nn.softplus(-lam.astype(jnp.float32))
    a = jnp.exp(log_a)
    mult = jnp.sqrt(jnp.maximum(1.0 - jnp.exp(2.0 * log_a), 0.0))
    pos = jnp.arange(S)[None, :, None]
    start = S - 1 if reverse else 0
    mult = jnp.where(pos == start, 1.0, mult)
    bx = mult * (i * xa)

    def combine(e1, e2):
        a1, b1 = e1
        a2, b2 = e2
        return a1 * a2, a2 * b1 + b2

    _, h = lax.associative_scan(combine, (a, bx), axis=1, reverse=reverse)
    return h


def hyena_filters(L, w1, b1, w2, b2, w3, freq):
    f32 = jnp.float32
    t = jnp.linspace(0.0, 1.0, L, dtype=f32)[:, None]
    w = (2.0 * math.pi / L) * jnp.arange(L, dtype=f32)[:, None]
    f = jnp.linspace(1e-4, FILTER_BANDS - 1, FILTER_BANDS, dtype=f32)[None, :]
    z = jnp.concatenate([t, jnp.cos(w * f), -jnp.sin(w * f)], axis=-1)
    freq = freq.astype(f32)
    h = jnp.sin(freq[0] * (z @ w1.astype(f32) + b1.astype(f32)))
    h = jnp.sin(freq[1] * (h @ w2.astype(f32) + b2.astype(f32)))
    h = (h @ w3.astype(f32)).reshape(L, HYENA_ORDER, 2, HYENA_WIDTH)
    min_decay = math.log(DECAY_TARGET) / DECAY_SLOW
    max_decay = math.log(DECAY_TARGET) / DECAY_FAST
    deltas = jnp.abs(jnp.linspace(min_decay, max_decay, HYENA_WIDTH, dtype=f32))
    decay = jnp.exp(-t * deltas[None, :])
    return h * decay[:, None, None, :]


def bidir_kernel(hf, hb):
    zero = jnp.zeros((1, hf.shape[1]), hf.dtype)
    return jnp.concatenate([hf, zero, hb[:0:-1]], axis=0)


def fftconv(u, k, bias):
    L = u.shape[1]
    U = jnp.fft.rfft(u, n=2 * L, axis=1)
    K = jnp.fft.rfft(k, axis=0)
    y = jnp.fft.irfft(U * K[None], n=2 * L, axis=1)[:, :L]
    return y + u * bias.astype(jnp.float32)


def setup_inputs(seed: int = 0) -> dict:
    key = jax.random.key(seed)
    ks = jax.random.split(key, 32)
    f32 = jnp.float32
    nrm = lambda k, shape, s: (jax.random.normal(k, shape, f32) * s).astype(f32)
    Dp = DEPTH
    u = jax.random.uniform(ks[9], (Dp, 2, LRU_WIDTH), f32, 0.9, 0.999)
    a0 = u ** (1.0 / LRU_C)
    lam = jnp.log(a0) - jnp.log1p(-a0)
    return {
        "x": nrm(ks[0], (BATCH, SEQ, D_MODEL), 1.0),
        "g_mix": 1.0 + nrm(ks[1], (Dp, D_MODEL), 0.01),
        "w_in": nrm(ks[2], (Dp, D_MODEL, IN_COLS), D_MODEL ** -0.5),
        "conv_a_w": nrm(ks[3], (Dp, LRU_CONV, LRU_WIDTH), LRU_CONV ** -0.5),
        "conv_a_b": nrm(ks[4], (Dp, LRU_WIDTH), 0.01),
        "lru_w_r": nrm(ks[5], (Dp, 2, LRU_HEADS, LRU_HEAD_DIM, LRU_HEAD_DIM), LRU_HEAD_DIM ** -0.5),
        "lru_b_r": nrm(ks[6], (Dp, 2, LRU_WIDTH), 0.01),
        "lru_w_i": nrm(ks[7], (Dp, 2, LRU_HEADS, LRU_HEAD_DIM, LRU_HEAD_DIM), LRU_HEAD_DIM ** -0.5),
        "lru_b_i": nrm(ks[8], (Dp, 2, LRU_WIDTH), 0.01),
        "lru_lambda": lam,
        "w_a_out": nrm(ks[10], (Dp, LRU_WIDTH, D_MODEL), LRU_WIDTH ** -0.5),
        "conv_b_w": nrm(ks[11], (Dp, HYENA_CONV, (HYENA_ORDER + 1) * HYENA_WIDTH), HYENA_CONV ** -0.5),
        "conv_b_b": nrm(ks[12], (Dp, (HYENA_ORDER + 1) * HYENA_WIDTH), 0.01),
        "filt_w1": nrm(ks[13], (Dp, FILTER_EMB, FILTER_HIDDEN), FILTER_EMB ** -0.5),
        "filt_b1": nrm(ks[14], (Dp, FILTER_HIDDEN), 0.1),
        "filt_w2": nrm(ks[15], (Dp, FILTER_HIDDEN, FILTER_HIDDEN), FILTER_HIDDEN ** -0.5),
        "filt_b2": nrm(ks[16], (Dp, FILTER_HIDDEN), 0.1),
        "filt_w3": nrm(ks[17], (Dp, FILTER_HIDDEN, HYENA_ORDER * 2 * HYENA_WIDTH), 0.005),
        "filt_freq": 1.0 + nrm(ks[18], (Dp, 2, FILTER_HIDDEN), 0.01),
        "filt_bias": nrm(ks[19], (Dp, HYENA_ORDER, HYENA_WIDTH), 0.5),
        "w_b_out": nrm(ks[20], (Dp, HYENA_WIDTH, D_MODEL), HYENA_WIDTH ** -0.5),
        "w_o": nrm(ks[21], (Dp, D_MODEL, D_MODEL), D_MODEL ** -0.5),
        "g_ffn": 1.0 + nrm(ks[22], (Dp, D_MODEL), 0.01),
        "w_router": nrm(ks[23], (Dp, D_MODEL, N_EXPERTS), D_MODEL ** -0.5),
        "w_gate": nrm(ks[24], (Dp, N_EXPERTS, D_MODEL, D_FF_EXPERT), D_MODEL ** -0.5),
        "w_up": nrm(ks[25], (Dp, N_EXPERTS, D_MODEL, D_FF_EXPERT), D_MODEL ** -0.5),
        "w_down": nrm(ks[26], (Dp, N_EXPERTS, D_FF_EXPERT, D_MODEL), D_FF_EXPERT ** -0.5),
        "g_final": 1.0 + nrm(ks[27], (D_MODEL,), 0.01),
    }


def reference(x, g_mix, w_in, conv_a_w, conv_a_b, lru_w_r, lru_b_r, lru_w_i, lru_b_i, lru_lambda, w_a_out,
              conv_b_w, conv_b_b, filt_w1, filt_b1, filt_w2, filt_b2, filt_w3, filt_freq, filt_bias, w_b_out,
              w_o, g_ffn, w_router, w_gate, w_up, w_down, g_final):
    f32 = jnp.float32
    dt = x.dtype
    B, S, D = x.shape
    bidx = jnp.arange(B)[:, None, None]
    cap = CAPACITY_FACTOR * S // N_EXPERTS
    for l in range(DEPTH):
        h = rmsnorm(x, g_mix[l])
        proj = h @ w_in[l]
        a_x, a_gate, hy, gates = jnp.split(
            proj, [LRU_WIDTH, 2 * LRU_WIDTH, 2 * LRU_WIDTH + (HYENA_ORDER + 1) * HYENA_WIDTH], axis=-1)

        xa = dwconv(a_x, conv_a_w[l], conv_a_b[l], (LRU_CONV // 2, LRU_CONV - 1 - LRU_CONV // 2)).astype(f32)
        h_a = (rg_lru(xa, lru_w_r[l, 0], lru_b_r[l, 0], lru_w_i[l, 0], lru_b_i[l, 0], lru_lambda[l, 0], False)
               + rg_lru(xa, lru_w_r[l, 1], lru_b_r[l, 1], lru_w_i[l, 1], lru_b_i[l, 1], lru_lambda[l, 1], True))
        y_a = (h_a * jax.nn.gelu(a_gate.astype(f32))).astype(dt) @ w_a_out[l]

        hy = dwconv(hy, conv_b_w[l], conv_b_b[l], (HYENA_CONV // 2, HYENA_CONV - 1 - HYENA_CONV // 2))
        v, x1, x2 = jnp.split(hy, HYENA_ORDER + 1, axis=-1)
        filt = hyena_filters(S, filt_w1[l], filt_b1[l], filt_w2[l], filt_b2[l], filt_w3[l], filt_freq[l])
        z = v.astype(f32)
        for n, gate in enumerate((x1, x2)):
            k = bidir_kernel(filt[:, n, 0], filt[:, n, 1])
            z = gate.astype(f32) * fftconv(z, k, filt_bias[l, n])
        y_b = z.astype(dt) @ w_b_out[l]

        g_a, g_b = jnp.split(jax.nn.sigmoid(gates.astype(f32)), N_BRANCHES, axis=-1)
        merged = (g_a * y_a.astype(f32) + g_b * y_b.astype(f32)).astype(dt)
        x = x + merged @ w_o[l]

        h = rmsnorm(x, g_ffn[l])
        aff = jax.nn.softmax((h @ w_router[l]).astype(f32), axis=-1)
        vals, idx = lax.top_k(jnp.swapaxes(aff, 1, 2), cap)
        xe = h[bidx, idx]
        gt = jnp.einsum('becd,edf->becf', xe, w_gate[l])
        up = jnp.einsum('becd,edf->becf', xe, w_up[l])
        ye = jnp.einsum('becf,efd->becd', jax.nn.silu(gt) * up, w_down[l])
        ye = ye * vals[..., None].astype(ye.dtype)
        x = x + jnp.zeros_like(x).at[bidx, idx].add(ye.astype(dt))
    return rmsnorm(x, g_final)
```

```python
import functools
import math

import numpy as np
import jax
import jax.numpy as jnp
from jax import lax
from jax.experimental import pallas as pl
from jax.experimental.pallas import tpu as pltpu

F32 = jnp.float32
BF16 = jnp.bfloat16
EPS = 1e-6

VMEM_LIMIT_BYTES = 56 * 1024 * 1024
BF16_ROWS = 16
F32_ROWS = 8

LRU_HEADS = 16
LRU_C = 8.0
N_EXPERTS = 16
CAPACITY_FACTOR = 2
FILTER_EMB = 33
FILTER_BANDS = 16
DECAY_FAST = 0.3
DECAY_SLOW = 1.5
DECAY_TARGET = 1e-2

FFT_N1 = 64
FFT_N2 = 128
N_FFT = FFT_N1 * FFT_N2
FFT_KH = FFT_N1 // 2 + 1
FFT_G = BF16_ROWS


def _params(sem):
    return pltpu.CompilerParams(dimension_semantics=sem, vmem_limit_bytes=VMEM_LIMIT_BYTES)


def _inproj_kernel(x_ref, g_ref, w_ref, o_ref, h_ref):
    @pl.when(pl.program_id(1) == 0)
    def _():
        x = x_ref[...]
        y = x * lax.rsqrt(jnp.mean(x * x, axis=-1, keepdims=True) + EPS)
        h_ref[...] = (y * g_ref[...]).astype(BF16)

    o_ref[...] = jnp.dot(h_ref[...], w_ref[...], preferred_element_type=F32).astype(o_ref.dtype)


def _inproj(x2, g, w_bf, tm=1024, tn=1024):
    T, D = x2.shape
    N = w_bf.shape[1]
    return pl.pallas_call(
        _inproj_kernel,
        grid=(T // tm, N // tn),
        in_specs=[
            pl.BlockSpec((tm, D), lambda i, j: (i, 0)),
            pl.BlockSpec((1, D), lambda i, j: (0, 0)),
            pl.BlockSpec((D, tn), lambda i, j: (0, j)),
        ],
        out_specs=pl.BlockSpec((tm, tn), lambda i, j: (i, j)),
        out_shape=jax.ShapeDtypeStruct((T, N), BF16),
        scratch_shapes=[pltpu.VMEM((tm, D), BF16)],
        compiler_params=_params(("parallel", "arbitrary")),
        name="inproj",
    )(x2, g.reshape(1, D), w_bf)


PAD = BF16_ROWS


def _stage_padded(src_ref, pad_ref):
    S, W = src_ref.shape
    pad_ref[0:PAD, :] = jnp.zeros((PAD, W), pad_ref.dtype)
    pad_ref[PAD + S:PAD + S + PAD, :] = jnp.zeros((PAD, W), pad_ref.dtype)
    pad_ref[PAD:PAD + S, :] = src_ref[...]


def _conv_chunk(pad_ref, r0, rows, taps, w_ref, b_ref):
    n = rows + 2 * PAD
    win = pad_ref[pl.ds(r0, n), :].astype(F32)
    acc = None
    for k, off in enumerate(taps):
        sh = win if off == 0 else pltpu.roll(win, (-off) % n, 0)
        term = sh[PAD:PAD + rows, :] * w_ref[k:k + 1, :]
        acc = term if acc is None else acc + term
    return acc + b_ref[...]


def _sigmoid(x):
    return 1.0 / (1.0 + jnp.exp(-x))


def _gelu_tanh(x):
    return 0.5 * x * (1.0 + jnp.tanh(math.sqrt(2.0 / math.pi) * (x + 0.044715 * (x * x * x))))


def _softplus(x):
    return jnp.maximum(x, 0.0) + jnp.log(1.0 + jnp.exp(-jnp.abs(x)))


def _tile_scan(a, b, carry, row, reverse):
    n = F32_ROWS
    for d in (1, 2, 4):
        if reverse:
            a_s = pltpu.roll(a, n - d, 0)
            b_s = pltpu.roll(b, n - d, 0)
            m = row < n - d
        else:
            a_s = pltpu.roll(a, d, 0)
            b_s = pltpu.roll(b, d, 0)
            m = row >= d
        b = jnp.where(m, a * b_s + b, b)
        a = jnp.where(m, a * a_s, a)
    h = a * carry + b
    edge = h[0:1, :] if reverse else h[n - 1:n, :]
    return h, jnp.broadcast_to(edge, h.shape)


def _rglru_kernel(ax_ref, gate_ref, cw_ref, cb_ref, wcat_ref, bias_ref, lam_ref, o_ref,
                  pad_ref, af_ref, bf_ref, ab_ref, bb_ref, *, chunk):
    S, W = ax_ref.shape
    _stage_padded(ax_ref, pad_ref)
    coef = [LRU_C * _softplus(-lam_ref[d:d + 1, :]) for d in range(2)]

    def gates(c, _):
        r0 = pl.multiple_of(c * chunk, chunk)
        xa = _conv_chunk(pad_ref, r0, chunk, (-2, -1, 0, 1), cw_ref, cb_ref)
        pre = jnp.dot(xa.astype(BF16), wcat_ref[...], preferred_element_type=F32)
        pos = r0 + lax.broadcasted_iota(jnp.int32, (chunk, W), 0)
        for d, (a_ref, b_ref, start) in enumerate(((af_ref, bf_ref, 0), (ab_ref, bb_ref, S - 1))):
            r = _sigmoid(pre[:, (2 * d) * W:(2 * d + 1) * W] + bias_ref[2 * d:2 * d + 1, :])
            i = _sigmoid(pre[:, (2 * d + 1) * W:(2 * d + 2) * W] + bias_ref[2 * d + 1:2 * d + 2, :])
            a = jnp.exp(-coef[d] * r)
            mult = jnp.sqrt(jnp.maximum(1.0 - a * a, 0.0))
            mult = jnp.where(pos == start, 1.0, mult)
            a_ref[pl.ds(r0, chunk), :] = a
            b_ref[pl.ds(r0, chunk), :] = mult * (i * xa)
        return 0

    lax.fori_loop(0, S // chunk, gates, 0)

    row = lax.broadcasted_iota(jnp.int32, (F32_ROWS, W), 0)
    ntile = S // F32_ROWS

    def fwd(t, carry):
        r0 = pl.multiple_of(t * F32_ROWS, F32_ROWS)
        h, carry = _tile_scan(af_ref[pl.ds(r0, F32_ROWS), :], bf_ref[pl.ds(r0, F32_ROWS), :], carry, row, False)
        bf_ref[pl.ds(r0, F32_ROWS), :] = h
        return carry

    lax.fori_loop(0, ntile, fwd, jnp.zeros((F32_ROWS, W), F32), unroll=4)

    def bwd(u, carry):
        r0 = pl.multiple_of((ntile // 2 - 1 - u) * BF16_ROWS, BF16_ROWS)
        hs = []
        for half in (1, 0):
            rr = r0 + half * F32_ROWS
            h, carry = _tile_scan(ab_ref[pl.ds(rr, F32_ROWS), :], bb_ref[pl.ds(rr, F32_ROWS), :], carry, row, True)
            hs.append(h + bf_ref[pl.ds(rr, F32_ROWS), :])
        hsum = jnp.concatenate([hs[1], hs[0]], axis=0)
        g = gate_ref[pl.ds(r0, BF16_ROWS), :].astype(F32)
        o_ref[pl.ds(r0, BF16_ROWS), :] = (hsum * _gelu_tanh(g)).astype(o_ref.dtype)
        return carry

    lax.fori_loop(0, ntile // 2, bwd, jnp.zeros((F32_ROWS, W), F32), unroll=2)


def _rglru(proj3, conv_w, conv_b, w_r, b_r, w_i, b_i, lam, wt=256, chunk=512):
    B, S, _ = proj3.shape
    W = conv_w.shape[1]
    nj = W // wt
    hd = W // LRU_HEADS
    hpt = wt // hd

    def blockdiag(w):
        w4 = w.reshape(nj, hpt, hd, hd)
        eye = jnp.eye(hpt, dtype=w.dtype)
        return jnp.einsum('jhab,hg->jhagb', w4, eye).reshape(nj, wt, wt)

    wcat = jnp.concatenate([blockdiag(w_r[0]), blockdiag(w_i[0]), blockdiag(w_r[1]), blockdiag(w_i[1])],
                           axis=-1).astype(BF16)
    bias = jnp.stack([b_r[0], b_i[0], b_r[1], b_i[1]], axis=0)
    gate_off = W // wt
    return pl.pallas_call(
        functools.partial(_rglru_kernel, chunk=chunk),
        grid=(B, nj),
        in_specs=[
            pl.BlockSpec((None, S, wt), lambda b, j: (b, 0, j)),
            pl.BlockSpec((None, S, wt), lambda b, j: (b, 0, gate_off + j)),
            pl.BlockSpec((conv_w.shape[0], wt), lambda b, j: (0, j)),
            pl.BlockSpec((1, wt), lambda b, j: (0, j)),
            pl.BlockSpec((None, wt, 4 * wt), lambda b, j: (j, 0, 0)),
            pl.BlockSpec((4, wt), lambda b, j: (0, j)),
            pl.BlockSpec((2, wt), lambda b, j: (0, j)),
        ],
        out_specs=pl.BlockSpec((None, S, wt), lambda b, j: (b, 0, j)),
        out_shape=jax.ShapeDtypeStruct((B, S, W), BF16),
        scratch_shapes=[pltpu.VMEM((S + 2 * PAD, wt), BF16)] + [pltpu.VMEM((S, wt), F32)] * 4,
        compiler_params=_params(("parallel", "parallel")),
        name="rglru",
    )(proj3, proj3, conv_w, conv_b.reshape(1, W), wcat, bias, lam)


@functools.lru_cache(maxsize=None)
def _dft_constants():
    g = FFT_G
    eye = np.eye(g)

    def stage_a(n1_in):
        k1 = np.arange(FFT_KH)[:, None]
        n1 = np.arange(n1_in)[None, :]
        ang = -2.0 * np.pi * ((k1 * n1) % FFT_N1) / FFT_N1
        a = np.stack([np.cos(ang), np.sin(ang)])
        return np.einsum('ckn,st->cksnt', a, eye).reshape(2 * FFT_KH * g, n1_in * g)

    k1 = np.arange(FFT_KH)[:, None, None]
    k2 = np.arange(FFT_N2)[None, :, None]
    n2 = np.arange(FFT_N2)[None, None, :]
    ang = -2.0 * np.pi * ((n2 * (k1 + FFT_N1 * k2)) % N_FFT) / N_FFT
    gr, gi = np.cos(ang), np.sin(ang)
    gb = np.concatenate([np.concatenate([gr, -gi], axis=2), np.concatenate([gi, gr], axis=2)], axis=1)
    gbi = np.transpose(gb, (0, 2, 1))

    c = np.full(FFT_KH, 2.0)
    c[0] = 1.0
    c[-1] = 1.0
    n1 = np.arange(FFT_N1 // 2)[:, None]
    kk = np.arange(FFT_KH)[None, :]
    ang = 2.0 * np.pi * ((n1 * kk) % FFT_N1) / FFT_N1
    bi = np.stack([np.cos(ang) * c / N_FFT, -np.sin(ang) * c / N_FFT])
    mai = np.einsum('cnk,st->nsckt', bi, eye).reshape((FFT_N1 // 2) * g, 2 * FFT_KH * g)
    f = np.float32
    return stage_a(FFT_N1 // 2).astype(f), stage_a(FFT_N1).astype(f), gb.astype(f), gbi.astype(f), mai.astype(f)


def _stage_a_fwd(src_ref, ma_ref, y_ref, n1_in):
    g = FFT_G
    for grp in range(FFT_N2 // g):
        tiles = [src_ref[FFT_N2 * n1 + g * grp:FFT_N2 * n1 + g * grp + g, :] for n1 in range(n1_in)]
        out = jnp.dot(ma_ref[...], jnp.concatenate(tiles, axis=0), preferred_element_type=F32)
        for c in range(2):
            for k1 in range(FFT_KH):
                r = (c * FFT_KH + k1) * g
                y_ref[k1, c * FFT_N2 + g * grp:c * FFT_N2 + g * grp + g, :] = out[r:r + g, :].astype(y_ref.dtype)


def _stage_b_fwd_slab(gb_ref, y_ref, k1):
    return jnp.dot(gb_ref[k1], y_ref[k1], preferred_element_type=F32)


def _filtfft_kernel(feat_ref, t_ref, w1_ref, b1_ref, w2_ref, b2_ref, w3f_ref, w3b_ref, freq_ref, delta_ref,
                    ma_ref, gb_ref, o_ref, kc_ref, y_ref, *, chunk):
    n, W = kc_ref.shape
    L = n // 2
    hp = lax.Precision.HIGHEST

    def body(c, _):
        r0 = pl.multiple_of(c * chunk, chunk)
        z = feat_ref[pl.ds(r0, chunk), :]
        h = jnp.sin(freq_ref[0:1, :] * (jnp.dot(z, w1_ref[...], precision=hp, preferred_element_type=F32) + b1_ref[...]))
        h = jnp.sin(freq_ref[1:2, :] * (jnp.dot(h, w2_ref[...], precision=hp, preferred_element_type=F32) + b2_ref[...]))
        hf = jnp.dot(h, w3f_ref[...], precision=hp, preferred_element_type=F32)
        hb = jnp.dot(h, w3b_ref[...], precision=hp, preferred_element_type=F32)
        row = r0 + lax.broadcasted_iota(jnp.int32, (chunk, W), 0)
        k = jnp.where(row < L, hf, jnp.where(row > L, hb, 0.0))
        decay = jnp.exp(-t_ref[pl.ds(r0, chunk), :] * delta_ref[...])
        kc_ref[pl.ds(r0, chunk), :] = (k * decay).astype(kc_ref.dtype)
        return 0

    lax.fori_loop(0, n // chunk, body, 0)
    _stage_a_fwd(kc_ref, ma_ref, y_ref, FFT_N1)

    def slab(k1, _):
        o_ref[k1] = _stage_b_fwd_slab(gb_ref, y_ref, k1).astype(o_ref.dtype)
        return 0

    lax.fori_loop(0, FFT_KH, slab, 0)


def _filter_spectra(L, w1, b1, w2, b2, w3, freq, W, ma64, gb, wt=256, chunk=1024):
    f32 = F32
    order = w3.shape[1] // (2 * W)
    hid = w1.shape[1]
    t = jnp.linspace(0.0, 1.0, L, dtype=f32)[:, None]
    w = (2.0 * math.pi / L) * jnp.arange(L, dtype=f32)[:, None]
    f = jnp.linspace(1e-4, FILTER_BANDS - 1, FILTER_BANDS, dtype=f32)[None, :]
    z = jnp.concatenate([t, jnp.cos(w * f), -jnp.sin(w * f)], axis=-1)
    emb = z.shape[1]
    embp = 128
    z = jnp.pad(z, ((0, 0), (0, embp - emb)))
    feat = jnp.concatenate([z, z[:1], z[:0:-1]], axis=0)
    tt = jnp.concatenate([t, t[:1], t[:0:-1]], axis=0)
    w1p = jnp.pad(w1, ((0, embp - emb), (0, 0)))
    min_decay = math.log(DECAY_TARGET) / DECAY_SLOW
    max_decay = math.log(DECAY_TARGET) / DECAY_FAST
    deltas = jnp.abs(jnp.linspace(min_decay, max_decay, W, dtype=f32))[None, :]
    nj = W // wt
    n = 2 * L
    const = lambda *shape: pl.BlockSpec(shape, lambda o, j: (0,) * len(shape))
    return pl.pallas_call(
        functools.partial(_filtfft_kernel, chunk=chunk),
        grid=(order, nj),
        in_specs=[
            const(n, embp), const(n, 1), const(embp, hid), const(1, hid), const(hid, hid), const(1, hid),
            pl.BlockSpec((hid, wt), lambda o, j: (0, (2 * o) * nj + j)),
            pl.BlockSpec((hid, wt), lambda o, j: (0, (2 * o + 1) * nj + j)),
            const(2, hid),
            pl.BlockSpec((1, wt), lambda o, j: (0, j)),
            const(*ma64.shape), const(*gb.shape),
        ],
        out_specs=pl.BlockSpec((None, None, FFT_KH, 2 * FFT_N2, wt), lambda o, j: (o, j, 0, 0, 0)),
        out_shape=jax.ShapeDtypeStruct((order, nj, FFT_KH, 2 * FFT_N2, wt), BF16),
        scratch_shapes=[pltpu.VMEM((n, wt), BF16), pltpu.VMEM((FFT_KH, 2 * FFT_N2, wt), BF16)],
        compiler_params=_params(("parallel", "parallel")),
        name="filtfft",
    )(feat, tt, w1p, b1.reshape(1, hid), w2, b2.reshape(1, hid), w3, w3, freq, deltas, ma64, gb)


def _hyena_kernel(v_ref, x1_ref, x2_ref, cw_ref, cb_ref, fb_ref, kf_ref, ma_ref, gb_ref, gbi_ref, mai_ref, o_ref,
                  pad_ref, z_ref, g1_ref, g2_ref, y_ref, *, chunk):
    S, W = v_ref.shape
    g = FFT_G

    for idx, (src, dst) in enumerate(((v_ref, z_ref), (x1_ref, g1_ref), (x2_ref, g2_ref))):
        _stage_padded(src, pad_ref)

        def conv(c, _, idx=idx, dst=dst):
            r0 = pl.multiple_of(c * chunk, chunk)
            y = _conv_chunk(pad_ref, r0, chunk, (-1, 0, 1), cw_ref.at[idx], cb_ref.at[idx])
            dst[pl.ds(r0, chunk), :] = y.astype(dst.dtype)
            return 0

        lax.fori_loop(0, S // chunk, conv, 0)

    for order, (gate_ref, dst) in enumerate(((g1_ref, z_ref), (g2_ref, o_ref))):
        _stage_a_fwd(z_ref, ma_ref, y_ref, FFT_N1 // 2)

        def slab(k1, _, order=order):
            x = _stage_b_fwd_slab(gb_ref, y_ref, k1)
            kf = kf_ref[order, k1].astype(F32)
            xr, xi = x[:FFT_N2], x[FFT_N2:]
            kr, ki = kf[:FFT_N2], kf[FFT_N2:]
            p = jnp.concatenate([xr * kr - xi * ki, xr * ki + xi * kr], axis=0).astype(BF16)
            y_ref[k1] = jnp.dot(gbi_ref[k1], p, preferred_element_type=F32).astype(y_ref.dtype)
            return 0

        lax.fori_loop(0, FFT_KH, slab, 0)

        bias = fb_ref[order:order + 1, :]
        for grp in range(FFT_N2 // g):
            tiles = [y_ref[k1, c * FFT_N2 + g * grp:c * FFT_N2 + g * grp + g, :]
                     for c in range(2) for k1 in range(FFT_KH)]
            out = jnp.dot(mai_ref[...], jnp.concatenate(tiles, axis=0), preferred_element_type=F32)
            for n1 in range(FFT_N1 // 2):
                t0 = FFT_N2 * n1 + g * grp
                zt = z_ref[t0:t0 + g, :].astype(F32)
                gt = gate_ref[t0:t0 + g, :].astype(F32)
                dst[t0:t0 + g, :] = (gt * (out[n1 * g:(n1 + 1) * g, :] + bias * zt)).astype(dst.dtype)


def _hyena(proj3, col0, conv_w, conv_b, filt_bias, kf, ma32, gb, gbi, mai, wt=256, chunk=512):
    B, S, _ = proj3.shape
    W = filt_bias.shape[1]
    nj = W // wt
    c0 = col0 // wt
    cw = conv_w.reshape(conv_w.shape[0], 3, W).transpose(1, 0, 2)
    cb = conv_b.reshape(3, 1, W)
    const = lambda *shape: pl.BlockSpec(shape, lambda j, b: (0,) * len(shape), pipeline_mode=pl.Buffered(1))
    return pl.pallas_call(
        functools.partial(_hyena_kernel, chunk=chunk),
        grid=(nj, B),
        in_specs=[
            pl.BlockSpec((None, S, wt), lambda j, b: (b, 0, c0 + j)),
            pl.BlockSpec((None, S, wt), lambda j, b: (b, 0, c0 + nj + j)),
            pl.BlockSpec((None, S, wt), lambda j, b: (b, 0, c0 + 2 * nj + j)),
            pl.BlockSpec((3, conv_w.shape[0], wt), lambda j, b: (0, 0, j)),
            pl.BlockSpec((3, 1, wt), lambda j, b: (0, 0, j)),
            pl.BlockSpec((filt_bias.shape[0], wt), lambda j, b: (0, j)),
            pl.BlockSpec((kf.shape[0], None, FFT_KH, 2 * FFT_N2, wt), lambda j, b: (0, j, 0, 0, 0),
                         pipeline_mode=pl.Buffered(1)),
            const(*ma32.shape), const(*gb.shape), const(*gbi.shape), const(*mai.shape),
        ],
        out_specs=pl.BlockSpec((None, S, wt), lambda j, b: (b, 0, j)),
        out_shape=jax.ShapeDtypeStruct((B, S, W), BF16),
        scratch_shapes=[pltpu.VMEM((S + 2 * PAD, wt), BF16)] + [pltpu.VMEM((S, wt), BF16)] * 3
        + [pltpu.VMEM((FFT_KH, 2 * FFT_N2, wt), BF16)],
        compiler_params=_params(("parallel", "arbitrary")),
        name="hyena",
    )(proj3, proj3, proj3, cw, cb, filt_bias, kf, ma32, gb, gbi, mai)


def _merge_kernel(ua_ref, zb_ref, ga_ref, gbr_ref, x_ref, wa_ref, wb_ref, wo_ref, g_ref, wrh_ref, wrl_ref,
                  x1_ref, h_ref, lg_ref):
    ya = jnp.dot(ua_ref[...], wa_ref[...], preferred_element_type=F32)
    yb = jnp.dot(zb_ref[...], wb_ref[...], preferred_element_type=F32)
    m = _sigmoid(ga_ref[...].astype(F32)) * ya + _sigmoid(gbr_ref[...].astype(F32)) * yb
    x1 = x_ref[...] + jnp.dot(m.astype(BF16), wo_ref[...], preferred_element_type=F32)
    x1_ref[...] = x1
    h = x1 * lax.rsqrt(jnp.mean(x1 * x1, axis=-1, keepdims=True) + EPS) * g_ref[...]
    hh = h.astype(BF16)
    h_ref[...] = hh
    hl = (h - hh.astype(F32)).astype(BF16)
    lg_ref[...] = (jnp.dot(hh, wrh_ref[...], preferred_element_type=F32)
                   + jnp.dot(hl, wrh_ref[...], preferred_element_type=F32)
                   + jnp.dot(hh, wrl_ref[...], preferred_element_type=F32))


def _merge(ua2, zb2, proj2, gate_col0, x2, wa, wb, wo, g_ffn, w_router, tm=512, epad=128):
    T, D = x2.shape
    gc = gate_col0 // D
    E = w_router.shape[1]
    wr = jnp.pad(w_router, ((0, 0), (0, epad - E)))
    wrh = wr.astype(BF16)
    wrl = (wr - wrh.astype(F32)).astype(BF16)
    row = lambda i: (i, 0)
    const = lambda *shape: pl.BlockSpec(shape, lambda i: (0,) * len(shape))
    return pl.pallas_call(
        _merge_kernel,
        grid=(T // tm,),
        in_specs=[
            pl.BlockSpec((tm, D), row), pl.BlockSpec((tm, D), row),
            pl.BlockSpec((tm, D), lambda i: (i, gc)), pl.BlockSpec((tm, D), lambda i: (i, gc + 1)),
            pl.BlockSpec((tm, D), row),
            const(D, D), const(D, D), const(D, D), const(1, D), const(D, epad), const(D, epad),
        ],
        out_specs=[pl.BlockSpec((tm, D), row), pl.BlockSpec((tm, D), row), pl.BlockSpec((tm, epad), row)],
        out_shape=[jax.ShapeDtypeStruct((T, D), F32), jax.ShapeDtypeStruct((T, D), BF16),
                   jax.ShapeDtypeStruct((T, epad), F32)],
        compiler_params=_params(("parallel",)),
        name="merge",
    )(ua2, zb2, proj2, proj2, x2, wa, wb, wo, g_ffn.reshape(1, D), wrh, wrl)


def _route_kernel(lg_ref, tri_ref, lmat_ref, affrow_ref, srow_ref, scol_ref, *, n_exp, cap):
    S, EP = lg_ref.shape
    nchunk = S // 128
    lane = lax.broadcasted_iota(jnp.int32, (S, EP), 1)
    lg = jnp.where(lane < n_exp, lg_ref[...], -1e30)
    e = jnp.exp(lg - jnp.max(lg, axis=-1, keepdims=True))
    aff = e / jnp.sum(e, axis=-1, keepdims=True)
    rows =[aff[c * 128:(c + 1) * 128, :].T[:n_exp, :] for c in range(nchunk)]
    ar = jnp.concatenate(rows, axis=1)
    affrow_ref[...] = ar

    def count_ge(thr):
        return jnp.sum(jnp.where(ar >= thr, 1.0, 0.0), axis=-1, keepdims=True)

    def cond(state):
        lo, hi, it = state
        mid = 0.5 * (lo + hi)
        open_ = jnp.logical_and(mid != lo, mid != hi)
        return jnp.logical_and(jnp.max(jnp.where(open_, 1.0, 0.0)) > 0.0, it < 400)

    def body(state):
        lo, hi, it = state
        mid = 0.5 * (lo + hi)
        ok = count_ge(mid) >= cap
        return jnp.where(ok, mid, lo), jnp.where(ok, hi, mid), it + 1

    lo0 = jnp.zeros((n_exp, 1), F32)
    hi0 = jnp.full((n_exp, 1), 2.0, F32)
    thr, _, _ = lax.while_loop(cond, body, (lo0, hi0, jnp.int32(0)))

    def cumsum_excl(x):
        xs = jnp.concatenate([x[:, c * 128:(c + 1) * 128] for c in range(nchunk)], axis=0)
        cs = jnp.dot(xs.astype(BF16), tri_ref[...], preferred_element_type=F32)
        off = jnp.dot(lmat_ref[...], cs.astype(BF16), preferred_element_type=F32)[:, 127:128]
        inc = cs + off
        return jnp.concatenate([inc[c * n_exp:(c + 1) * n_exp, :] for c in range(nchunk)], axis=1) - x

    gt = ar > thr
    eq = ar == thr
    need = cap - jnp.sum(jnp.where(gt, 1.0, 0.0), axis=-1, keepdims=True)
    tie_rank = cumsum_excl(jnp.where(eq, 1.0, 0.0))
    sel = jnp.logical_or(gt, jnp.logical_and(eq, tie_rank < need))
    self_ = jnp.where(sel, 1.0, 0.0)
    slot = jnp.where(sel, cumsum_excl(self_), -1.0)
    srow_ref[...] = slot.astype(jnp.int32)
    slot_p = jnp.concatenate([slot, jnp.full((EP - n_exp, S), -1.0, F32)], axis=0)
    cols = [slot_p[:, c * 128:(c + 1) * 128].T for c in range(nchunk)]
    scol_ref[...] = jnp.concatenate(cols, axis=0).astype(jnp.int32)


def _route(logits3, n_exp, cap):
    B, S, EP = logits3.shape
    nchunk = S // 128
    i = np.arange(128)
    tri = (i[:, None] <= i[None, :]).astype(np.float32)
    r = np.arange(nchunk * n_exp)
    lmat = ((r[:, None] % n_exp == r[None, :] % n_exp) & (r[None, :] // n_exp < r[:, None] // n_exp)).astype(np.float32)
    const = lambda *shape: pl.BlockSpec(shape, lambda b: (0,) * len(shape))
    return pl.pallas_call(
        functools.partial(_route_kernel, n_exp=n_exp, cap=cap),
        grid=(B,),
        in_specs=[pl.BlockSpec((None, S, EP), lambda b: (b, 0, 0)), const(128, 128), const(*lmat.shape)],
        out_specs=[
            pl.BlockSpec((None, n_exp, S), lambda b: (b, 0, 0)),
            pl.BlockSpec((None, n_exp, S), lambda b: (b, 0, 0)),
            pl.BlockSpec((None, S, EP), lambda b: (b, 0, 0)),
        ],
        out_shape=[
            jax.ShapeDtypeStruct((B, n_exp, S), F32),
            jax.ShapeDtypeStruct((B, n_exp, S), jnp.int32), jax.ShapeDtypeStruct((B, S, EP), jnp.int32),
        ],
        compiler_params=_params(("parallel",)),
        name="route",
    )(logits3, jnp.asarray(tri, BF16), jnp.asarray(lmat, BF16))


def _gather_kernel(h_ref, srow_ref, affrow_ref, xe_ref, val_ref, *, cap):
    e = pl.program_id(1)
    S = h_ref.shape[0]
    slot = srow_ref[pl.ds(e, 1), :]
    hit = lax.broadcasted_iota(jnp.int32, (cap, S), 0) == slot
    xe_ref[...] = jnp.dot(jnp.where(hit, 1.0, 0.0).astype(BF16), h_ref[...],
                          preferred_element_type=F32).astype(xe_ref.dtype)
    val_ref[...] = jnp.sum(jnp.where(hit, affrow_ref[pl.ds(e, 1), :], 0.0), axis=-1, keepdims=True)


def _gather(h3, srow, affrow, cap):
    B, S, D = h3.shape
    E = srow.shape[1]
    return pl.pallas_call(
        functools.partial(_gather_kernel, cap=cap),
        grid=(B, E),
        in_specs=[
            pl.BlockSpec((None, S, D), lambda b, e: (b, 0, 0)),
            pl.BlockSpec((None, E, S), lambda b, e: (b, 0, 0)),
            pl.BlockSpec((None, E, S), lambda b, e: (b, 0, 0)),
        ],
        out_specs=[
            pl.BlockSpec((None, None, cap, D), lambda b, e: (b, e, 0, 0)),
            pl.BlockSpec((None, None, cap, 1), lambda b, e: (b, e, 0, 0)),
        ],
        out_shape=[jax.ShapeDtypeStruct((B, E, cap, D), BF16), jax.ShapeDtypeStruct((B, E, cap, 1), F32)],
        compiler_params=_params(("parallel", "arbitrary")),
        name="gather",
    )(h3, srow, affrow)


def _ffn_kernel(xe_ref, val_ref, wg_ref, wu_ref, wd_ref, ye_ref):
    xe = xe_ref[...]
    gt = jnp.dot(xe, wg_ref[...], preferred_element_type=F32)
    up = jnp.dot(xe, wu_ref[...], preferred_element_type=F32)
    act = (gt * _sigmoid(gt) * up).astype(BF16)
    ye = jnp.dot(act, wd_ref[...], preferred_element_type=F32)
    ye_ref[...] = (ye * val_ref[...]).astype(ye_ref.dtype)


def _ffn(xe, vals, wg, wu, wd):
    B, E, C, D = xe.shape
    F = wg.shape[2]
    return pl.pallas_call(
        _ffn_kernel,
        grid=(E, B),
        in_specs=[
            pl.BlockSpec((None, None, C, D), lambda e, b: (b, e, 0, 0)),
            pl.BlockSpec((None, None, C, 1), lambda e, b: (b, e, 0, 0)),
            pl.BlockSpec((None, D, F), lambda e, b: (e, 0, 0)),
            pl.BlockSpec((None, D, F), lambda e, b: (e, 0, 0)),
            pl.BlockSpec((None, F, D), lambda e, b: (e, 0, 0)),
        ],
        out_specs=pl.BlockSpec((None, None, C, D), lambda e, b: (b, e, 0, 0)),
        out_shape=jax.ShapeDtypeStruct((B, E, C, D), BF16),
        compiler_params=_params(("parallel", "arbitrary")),
        name="ffn",
    )(xe, vals, wg, wu, wd)


def _combine_kernel(ye_ref, scol_ref, x1_ref, g_ref, o_ref, *, n_exp, cap):
    tm = x1_ref.shape[0]
    lane = lax.broadcasted_iota(jnp.int32, (tm, cap), 1)
    scol = scol_ref[...]
    acc = x1_ref[...]
    for e in range(n_exp):
        hit = jnp.where(lane == scol[:, e:e + 1], 1.0, 0.0).astype(BF16)
        acc = acc + jnp.dot(hit, ye_ref[e], preferred_element_type=F32)
    y = acc * lax.rsqrt(jnp.mean(acc * acc, axis=-1, keepdims=True) + EPS)
    o_ref[...] = y * g_ref[...]


def _combine(ye, scol, x13, g_final, tm=512):
    B, E, C, D = ye.shape
    S = x13.shape[1]
    EP = scol.shape[2]
    return pl.pallas_call(
        functools.partial(_combine_kernel, n_exp=E, cap=C),
        grid=(B, S // tm),
        in_specs=[
            pl.BlockSpec((None, E, C, D), lambda b, r: (b, 0, 0, 0)),
            pl.BlockSpec((None, tm, EP), lambda b, r: (b, r, 0)),
            pl.BlockSpec((None, tm, D), lambda b, r: (b, r, 0)),
            pl.BlockSpec((1, D), lambda b, r: (0, 0)),
        ],
        out_specs=pl.BlockSpec((None, tm, D), lambda b, r: (b, r, 0)),
        out_shape=jax.ShapeDtypeStruct((B, S, D), F32),
        compiler_params=_params(("parallel", "arbitrary")),
        name="combine",
    )(ye, scol, x13, g_final.reshape(1, D))


def kernel(x, g_mix, w_in, conv_a_w, conv_a_b, lru_w_r, lru_b_r, lru_w_i, lru_b_i, lru_lambda, w_a_out, conv_b_w, conv_b_b, filt_w1, filt_b1, filt_w2, filt_b2, filt_w3, filt_freq, filt_bias, w_b_out, w_o, g_ffn, w_router, w_gate, w_up, w_down, g_final):
    B, S, D = x.shape
    assert w_in.shape[0] == 1, "single-layer block only"
    l = 0
    lru_w = conv_a_w.shape[2]
    hy_w = filt_bias.shape[2]
    n_exp = w_router.shape[2]
    cap = CAPACITY_FACTOR * S // n_exp
    assert 2 * S == N_FFT
    ma32, ma64, gb, gbi, mai = (jnp.asarray(c).astype(BF16) for c in _dft_constants())

    x2 = x.reshape(B * S, D)
    proj2 = _inproj(x2, g_mix[l], w_in[l].astype(BF16))
    proj3 = proj2.reshape(B, S, -1)
    ua = _rglru(proj3, conv_a_w[l], conv_a_b[l], lru_w_r[l], lru_b_r[l], lru_w_i[l], lru_b_i[l], lru_lambda[l])
    kf = _filter_spectra(S, filt_w1[l], filt_b1[l], filt_w2[l], filt_b2[l], filt_w3[l], filt_freq[l], hy_w, ma64, gb)
    zb = _hyena(proj3, 2 * lru_w, conv_b_w[l], conv_b_b[l], filt_bias[l], kf, ma32, gb, gbi, mai)
    x1, h, logits = _merge(ua.reshape(B * S, lru_w), zb.reshape(B * S, hy_w), proj2, 2 * lru_w + 3 * hy_w, x2,
                           w_a_out[l].astype(BF16), w_b_out[l].astype(BF16), w_o[l].astype(BF16), g_ffn[l],
                           w_router[l])
    affrow, srow, scol = _route(logits.reshape(B, S, -1), n_exp, cap)
    xe, vals = _gather(h.reshape(B, S, D), srow, affrow, cap)
    ye = _ffn(xe, vals, w_gate[l].astype(BF16), w_up[l].astype(BF16), w_down[l].astype(BF16))
    return _combine(ye, scol, x1.reshape(B, S, D), g_final)
```

```python
import functools
import math

import numpy as np
import jax
import jax.numpy as jnp
from jax import lax
from jax.experimental import pallas as pl
from jax.experimental.pallas import tpu as pltpu

F32 = jnp.float32
BF16 = jnp.bfloat16
EPS = 1e-6
LOG2E = 1.4426950408889634

VMEM_LIMIT_BYTES = 56 * 1024 * 1024
BF16_ROWS = 16
F32_ROWS = 8
LANES = 128

LRU_HEADS = 16
LRU_C = 8.0
CAPACITY_FACTOR = 2
FILTER_BANDS = 16
DECAY_FAST = 0.3
DECAY_SLOW = 1.5
DECAY_TARGET = 1e-2

FFT_N1 = 64
FFT_N2 = 128
N_FFT = FFT_N1 * FFT_N2
FFT_KH = FFT_N1 // 2 + 1
FFT_G = F32_ROWS


def _params(sem):
    return pltpu.CompilerParams(dimension_semantics=sem, vmem_limit_bytes=VMEM_LIMIT_BYTES)


def _inproj_kernel(x_ref, g_ref, w_ref, o_ref, h_ref):
    @pl.when(pl.program_id(1) == 0)
    def _():
        x = x_ref[...]
        y = x * lax.rsqrt(jnp.mean(x * x, axis=-1, keepdims=True) + EPS)
        h_ref[...] = (y * g_ref[...]).astype(BF16)

    o_ref[...] = jnp.dot(h_ref[...], w_ref[...], preferred_element_type=F32).astype(o_ref.dtype)


def _inproj(x2, g, w_bf, tm=1024, tn=1024):
    T, D = x2.shape
    N = w_bf.shape[1]
    return pl.pallas_call(
        _inproj_kernel,
        grid=(T // tm, N // tn),
        in_specs=[
            pl.BlockSpec((tm, D), lambda i, j: (i, 0)),
            pl.BlockSpec((1, D), lambda i, j: (0, 0)),
            pl.BlockSpec((D, tn), lambda i, j: (0, j)),
        ],
        out_specs=pl.BlockSpec((tm, tn), lambda i, j: (i, j)),
        out_shape=jax.ShapeDtypeStruct((T, N), BF16),
        scratch_shapes=[pltpu.VMEM((tm, D), BF16)],
        compiler_params=_params(("parallel", "arbitrary")),
        name="inproj",
    )(x2, g.reshape(1, D), w_bf)


PAD = BF16_ROWS


def _stage_padded(src_ref, pad_ref):
    S, W = src_ref.shape
    pad_ref[0:PAD, :] = jnp.zeros((PAD, W), pad_ref.dtype)
    pad_ref[PAD + S:PAD + S + PAD, :] = jnp.zeros((PAD, W), pad_ref.dtype)
    pad_ref[PAD:PAD + S, :] = src_ref[...]


def _conv_chunk(pad_ref, r0, rows, taps, w_ref, b_ref):
    n = rows + 2 * PAD
    win = pad_ref[pl.ds(r0, n), :].astype(F32)
    acc = None
    for k, off in enumerate(taps):
        sh = win if off == 0 else pltpu.roll(win, (-off) % n, 0)
        term = sh[PAD:PAD + rows, :] * w_ref[k:k + 1, :]
        acc = term if acc is None else acc + term
    return acc + b_ref[...]


NSEG = F32_ROWS
SCAN_UNROLL = 8


def _sigmoid(x):
    return 1.0 / (1.0 + jnp.exp2(x * (-LOG2E)))


def _gelu_tanh(x):
    return 0.5 * x * (1.0 + jnp.tanh(math.sqrt(2.0 / math.pi) * (x + 0.044715 * (x * x * x))))


def _softplus(x):
    return jnp.maximum(x, 0.0) + jnp.log(1.0 + jnp.exp(-jnp.abs(x)))


def _ld(ref, rows):
    return jnp.concatenate([ref[l, rows, :] for l in range(ref.shape[0])], axis=1)


def _st(ref, rows, v):
    for l in range(ref.shape[0]):
        ref[l, rows, :] = v[:, l * LANES:(l + 1) * LANES]


def _rglru_kernel(ax_ref, gate_ref, cw_ref, cb_ref, wcat_ref, bias_ref, lam_ref, o_ref,
                  pad_ref, af_ref, bf_ref, ab_ref, bb_ref, hs_ref):
    S, W = ax_ref.shape
    seg = S // NSEG
    pitch = hs_ref.shape[1] // NSEG
    _stage_padded(ax_ref, pad_ref)
    coef = [(LRU_C * LOG2E) * _softplus(-lam_ref[d:d + 1, :]) for d in range(2)]

    def gates(c, _):
        r0 = pl.multiple_of(c * seg, seg)
        xa = _conv_chunk(pad_ref, r0, seg, (-2, -1, 0, 1), cw_ref, cb_ref)
        pre = jnp.dot(xa.astype(BF16), wcat_ref[...], preferred_element_type=F32)
        pos = r0 + lax.broadcasted_iota(jnp.int32, (seg, W), 0)
        for d, (a_ref, b_ref, start) in enumerate(((af_ref, bf_ref, 0), (ab_ref, bb_ref, S - 1))):
            r = _sigmoid(pre[:, (2 * d) * W:(2 * d + 1) * W] + bias_ref[2 * d:2 * d + 1, :])
            i = _sigmoid(pre[:, (2 * d + 1) * W:(2 * d + 2) * W] + bias_ref[2 * d + 1:2 * d + 2, :])
            a = jnp.exp2(-coef[d] * r)
            m2 = 1.0 - a * a
            mult = jnp.where(m2 > 0.0, m2 * lax.rsqrt(m2), 0.0)
            mult = jnp.where(pos == start, 1.0, mult)
            _st(a_ref, pl.ds(c, seg, stride=NSEG), a)
            _st(b_ref, pl.ds(c, seg, stride=NSEG), mult * (i * xa))
        return 0

    lax.fori_loop(0, NSEG, gates, 0)

    def scan(u, carry):
        hf, pf, hb, pb = carry
        for k in range(SCAN_UNROLL):
            j = u * SCAN_UNROLL + k
            rf = pl.multiple_of(j * NSEG, NSEG)
            rb = pl.multiple_of((seg - 1 - j) * NSEG, NSEG)
            a = _ld(af_ref, pl.ds(rf, NSEG))
            hf = a * hf + _ld(bf_ref, pl.ds(rf, NSEG))
            pf = a * pf
            _st(bf_ref, pl.ds(rf, NSEG), hf)
            _st(af_ref, pl.ds(rf, NSEG), pf)
            a = _ld(ab_ref, pl.ds(rb, NSEG))
            hb = a * hb + _ld(bb_ref, pl.ds(rb, NSEG))
            pb = a * pb
            _st(bb_ref, pl.ds(rb, NSEG), hb)
            _st(ab_ref, pl.ds(rb, NSEG), pb)
        return hf, pf, hb, pb

    zero = jnp.zeros((NSEG, W), F32)
    one = jnp.ones((NSEG, W), F32)
    hf, pf, hb, pb = lax.fori_loop(0, seg // SCAN_UNROLL, scan, (zero, one, zero, one))

    row = lax.broadcasted_iota(jnp.int32, (NSEG, W), 0)
    cf = zero
    cb = zero
    for _ in range(NSEG - 1):
        cf = jnp.where(row >= 1, pltpu.roll(hf + pf * cf, 1, 0), 0.0)
        cb = jnp.where(row < NSEG - 1, pltpu.roll(hb + pb * cb, NSEG - 1, 0), 0.0)

    def fix(u, _):
        for k in range(SCAN_UNROLL):
            j = u * SCAN_UNROLL + k
            r = pl.multiple_of(j * NSEG, NSEG)
            rows = pl.ds(r, NSEG)
            h = _ld(bf_ref, rows) + _ld(af_ref, rows) * cf + _ld(bb_ref, rows) + _ld(ab_ref, rows) * cb
            _st(hs_ref, pl.ds(j, NSEG, stride=pitch), h)
        return 0

    lax.fori_loop(0, seg // SCAN_UNROLL, fix, 0)

    def out(c, _):
        r0 = pl.multiple_of(c * seg, seg)
        h = _ld(hs_ref, pl.ds(pl.multiple_of(c * pitch, F32_ROWS), seg))
        g = gate_ref[pl.ds(r0, seg), :].astype(F32)
        o_ref[pl.ds(r0, seg), :] = (h * _gelu_tanh(g)).astype(o_ref.dtype)
        return 0

    lax.fori_loop(0, NSEG, out, 0)


def _rglru(proj3, conv_w, conv_b, w_r, b_r, w_i, b_i, lam, wt=256):
    B, S, _ = proj3.shape
    W = conv_w.shape[1]
    nj = W // wt
    hd = W // LRU_HEADS
    hpt = wt // hd
    pitch = S // NSEG + F32_ROWS

    def blockdiag(w):
        w4 = w.reshape(nj, hpt, hd, hd)
        eye = jnp.eye(hpt, dtype=w.dtype)
        return jnp.einsum('jhab,hg->jhagb', w4, eye).reshape(nj, wt, wt)

    wcat = jnp.concatenate([blockdiag(w_r[0]), blockdiag(w_i[0]), blockdiag(w_r[1]), blockdiag(w_i[1])],
                           axis=-1).astype(BF16)
    bias = jnp.stack([b_r[0], b_i[0], b_r[1], b_i[1]], axis=0)
    gate_off = W // wt
    return pl.pallas_call(
        _rglru_kernel,
        grid=(B, nj),
        in_specs=[
            pl.BlockSpec((None, S, wt), lambda b, j: (b, 0, j)),
            pl.BlockSpec((None, S, wt), lambda b, j: (b, 0, gate_off + j)),
            pl.BlockSpec((conv_w.shape[0], wt), lambda b, j: (0, j)),
            pl.BlockSpec((1, wt), lambda b, j: (0, j)),
            pl.BlockSpec((None, wt, 4 * wt), lambda b, j: (j, 0, 0)),
            pl.BlockSpec((4, wt), lambda b, j: (0, j)),
            pl.BlockSpec((2, wt), lambda b, j: (0, j)),
        ],
        out_specs=pl.BlockSpec((None, S, wt), lambda b, j: (b, 0, j)),
        out_shape=jax.ShapeDtypeStruct((B, S, W), BF16),
        scratch_shapes=[pltpu.VMEM((S + 2 * PAD, wt), BF16)] + [pltpu.VMEM((wt // LANES, S, LANES), F32)] * 4
        + [pltpu.VMEM((wt // LANES, NSEG * pitch, LANES), F32)],
        compiler_params=_params(("parallel", "parallel")),
        name="rglru",
    )(proj3, proj3, conv_w, conv_b.reshape(1, W), wcat, bias, lam)


COMPS_FWD = [(c, k1) for c in range(2) for k1 in range(FFT_KH)]
COMPS_INV = [(c, k1) for (c, k1) in COMPS_FWD if not (c == 1 and k1 in (0, FFT_KH - 1))]


@functools.lru_cache(maxsize=None)
def _dft_constants():
    g = FFT_G
    eye = np.eye(g)

    def stage_a(n1_in):
        k1 = np.arange(FFT_KH)[:, None]
        n1 = np.arange(n1_in)[None, :]
        ang = -2.0 * np.pi * ((k1 * n1) % FFT_N1) / FFT_N1
        a = np.stack([np.cos(ang), np.sin(ang)])
        a = np.stack([a[c, k] for (c, k) in COMPS_FWD])
        return np.einsum('qn,st->qsnt', a, eye).reshape(len(COMPS_FWD) * g, n1_in * g)

    k1 = np.arange(FFT_KH)[:, None, None]
    k2 = np.arange(FFT_N2)[None, :, None]
    n2 = np.arange(FFT_N2)[None, None, :]
    ang = -2.0 * np.pi * ((n2 * (k1 + FFT_N1 * k2)) % N_FFT) / N_FFT
    gr, gi = np.cos(ang), np.sin(ang)
    gb = np.concatenate([np.concatenate([gr, -gi], axis=2), np.concatenate([gi, gr], axis=2)], axis=1)
    gbi = np.transpose(gb, (0, 2, 1))

    c = np.full(FFT_KH, 2.0)
    c[0] = 1.0
    c[-1] = 1.0
    n1 = np.arange(FFT_N1 // 2)[:, None]
    kk = np.arange(FFT_KH)[None, :]
    ang = 2.0 * np.pi * ((n1 * kk) % FFT_N1) / FFT_N1
    bi = np.stack([np.cos(ang) * c / N_FFT, -np.sin(ang) * c / N_FFT])
    bi = np.stack([bi[c_, :, k] for (c_, k) in COMPS_INV], axis=1)
    mai = np.einsum('nq,st->nsqt', bi, eye).reshape((FFT_N1 // 2) * g, len(COMPS_INV) * g)
    f = np.float32
    return stage_a(FFT_N1 // 2).astype(f), stage_a(FFT_N1).astype(f), gb.astype(f), gbi.astype(f), mai.astype(f)


def _split_rows(tiles):
    f = [t.astype(F32) for t in tiles]
    return [jnp.concatenate([t[h * FFT_G:(h + 1) * FFT_G] for t in f], axis=0).astype(BF16)
            for h in range(BF16_ROWS // FFT_G)]


def _join_rows(outs, q):
    return jnp.concatenate([o[q * FFT_G:(q + 1) * FFT_G] for o in outs], axis=0)


def _stage_a_fwd(src_ref, ma_ref, y_ref, n1_in):
    for p in range(FFT_N2 // BF16_ROWS):
        r = BF16_ROWS * p
        opnds = _split_rows([src_ref[FFT_N2 * n1 + r:FFT_N2 * n1 + r + BF16_ROWS, :] for n1 in range(n1_in)])
        outs = [jnp.dot(ma_ref[...], o, preferred_element_type=F32) for o in opnds]
        for q, (c, k1) in enumerate(COMPS_FWD):
            y_ref[k1, c * FFT_N2 + r:c * FFT_N2 + r + BF16_ROWS, :] = _join_rows(outs, q).astype(y_ref.dtype)


def _stage_a_inv(y_ref, mai_ref, p):
    r = BF16_ROWS * p
    opnds = _split_rows([y_ref[k1, c * FFT_N2 + r:c * FFT_N2 + r + BF16_ROWS, :] for (c, k1) in COMPS_INV])
    outs = [jnp.dot(mai_ref[...], o, preferred_element_type=F32) for o in opnds]
    return [_join_rows(outs, n1) for n1 in range(FFT_N1 // 2)]


def _stage_b_fwd_slab(gb_ref, y_ref, k1):
    return jnp.dot(gb_ref[k1], y_ref[k1], preferred_element_type=F32)


def _dot3(a, b):
    ah = a.astype(BF16)
    al = (a - ah.astype(F32)).astype(BF16)
    bh = b.astype(BF16)
    bl = (b - bh.astype(F32)).astype(BF16)
    d = lambda u, v: jnp.dot(u, v, preferred_element_type=F32)
    return d(ah, bh) + d(al, bh) + d(ah, bl)


def _filtmlp_kernel(fa_ref, fb_ref, w1_ref, b1_ref, w2_ref, b2_ref, freq_ref, o_ref):
    u = jnp.concatenate([_dot3(fa_ref[...], w1_ref[...]), _dot3(fb_ref[...], w1_ref[...])], axis=1)
    h = jnp.sin(freq_ref[0:1, :] * (u + b1_ref[...]))
    o_ref[...] = jnp.sin(freq_ref[1:2, :] * (_dot3(h, w2_ref[...]) + b2_ref[...]))


def _filtfft_kernel(h_ref, t_ref, w3f_ref, w3b_ref, delta_ref, ma_ref, gb_ref, o_ref, kc_ref, y_ref, *, chunk):
    n, W = kc_ref.shape
    L = n // 2

    def body(c, _):
        r0 = pl.multiple_of(c * chunk, chunk)
        h = h_ref[pl.ds(r0, chunk), :].astype(BF16)
        kf = jnp.dot(h, w3f_ref[...], preferred_element_type=F32)
        kb = jnp.dot(h, w3b_ref[...], preferred_element_type=F32)
        row = r0 + lax.broadcasted_iota(jnp.int32, (chunk, W), 0)
        kb = jnp.where(row == 0, 0.0, kb)
        kc_ref[pl.ds(r0, chunk), :] = (kf * jnp.exp(-t_ref[pl.ds(r0, chunk), :] * delta_ref[...])).astype(kc_ref.dtype)
        kc_ref[pl.ds(L + r0, chunk), :] = (kb * jnp.exp(-t_ref[pl.ds(L + r0, chunk), :] * delta_ref[...])).astype(kc_ref.dtype)
        return 0

    lax.fori_loop(0, L // chunk, body, 0)
    _stage_a_fwd(kc_ref, ma_ref, y_ref, FFT_N1)

    def slab(k1, _):
        o_ref[k1] = _stage_b_fwd_slab(gb_ref, y_ref, k1).astype(o_ref.dtype)
        return 0

    lax.fori_loop(0, FFT_KH, slab, 0, unroll=11)


def _filter_spectra(L, w1, b1, w2, b2, w3, freq, W, ma64, gb, wt=256, chunk=512):
    f32 = F32
    order = w3.shape[1] // (2 * W)
    hid = w1.shape[1]
    t = jnp.linspace(0.0, 1.0, L, dtype=f32)[:, None]
    w = (2.0 * math.pi / L) * jnp.arange(L, dtype=f32)[:, None]
    f = jnp.linspace(1e-4, FILTER_BANDS - 1, FILTER_BANDS, dtype=f32)[None, :]
    z = jnp.concatenate([t, jnp.cos(w * f), -jnp.sin(w * f)], axis=-1)
    emb = z.shape[1]
    embp = 128
    z = jnp.pad(z, ((0, 0), (0, embp - emb)))
    feat = jnp.concatenate([z, z[:1], z[:0:-1]], axis=0)
    tt = jnp.concatenate([t, t[:1], t[:0:-1]], axis=0)
    w1p = jnp.pad(w1, ((0, embp - emb), (0, 0)))
    zero = jnp.zeros_like(w2)
    w2d = jnp.concatenate([jnp.concatenate([w2, zero], axis=1), jnp.concatenate([zero, w2], axis=1)], axis=0)
    two = lambda v: jnp.concatenate([v, v], axis=-1)
    nblk = L // chunk
    const1 = lambda *shape: pl.BlockSpec(shape, lambda i: (0,) * len(shape))
    hfeat = pl.pallas_call(
        _filtmlp_kernel,
        grid=(nblk,),
        in_specs=[
            pl.BlockSpec((chunk, embp), lambda i: (i, 0)), pl.BlockSpec((chunk, embp), lambda i: (nblk + i, 0)),
            const1(embp, hid), const1(1, 2 * hid), const1(2 * hid, 2 * hid), const1(1, 2 * hid), const1(2, 2 * hid),
        ],
        out_specs=pl.BlockSpec((chunk, 2 * hid), lambda i: (i, 0)),
        out_shape=jax.ShapeDtypeStruct((L, 2 * hid), f32),
        compiler_params=_params(("parallel",)),
        name="filtmlp",
    )(feat, feat, w1p, two(b1.reshape(1, hid)), w2d, two(b2.reshape(1, hid)), two(freq))

    w3b16 = w3.astype(BF16)
    zero3 = jnp.zeros_like(w3b16)
    w3f = jnp.concatenate([w3b16, zero3], axis=0)
    w3b = jnp.concatenate([zero3, w3b16], axis=0)
    min_decay = math.log(DECAY_TARGET) / DECAY_SLOW
    max_decay = math.log(DECAY_TARGET) / DECAY_FAST
    deltas = jnp.abs(jnp.linspace(min_decay, max_decay, W, dtype=f32))[None, :]
    nj = W // wt
    n = 2 * L
    const = lambda *shape: pl.BlockSpec(shape, lambda o, j: (0,) * len(shape))
    return pl.pallas_call(
        functools.partial(_filtfft_kernel, chunk=chunk),
        grid=(order, nj),
        in_specs=[
            const(L, 2 * hid), const(n, 1),
            pl.BlockSpec((2 * hid, wt), lambda o, j: (0, (2 * o) * nj + j)),
            pl.BlockSpec((2 * hid, wt), lambda o, j: (0, (2 * o + 1) * nj + j)),
            pl.BlockSpec((1, wt), lambda o, j: (0, j)),
            const(*ma64.shape), const(*gb.shape),
        ],
        out_specs=pl.BlockSpec((None, None, FFT_KH, 2 * FFT_N2, wt), lambda o, j: (o, j, 0, 0, 0)),
        out_shape=jax.ShapeDtypeStruct((order, nj, FFT_KH, 2 * FFT_N2, wt), BF16),
        scratch_shapes=[pltpu.VMEM((n, wt), BF16), pltpu.VMEM((FFT_KH, 2 * FFT_N2, wt), BF16)],
        compiler_params=_params(("parallel", "parallel")),
        name="filtfft",
    )(hfeat, tt, w3f, w3b, deltas, ma64, gb)


def _hyena_kernel(v_ref, x1_ref, x2_ref, cw_ref, cb_ref, fb_ref, kf_ref, ma_ref, gb_ref, gbi_ref, mai_ref, o_ref,
                  pad_ref, z_ref, g1_ref, g2_ref, y_ref, *, chunk):
    S, W = v_ref.shape
    g = BF16_ROWS

    for idx, (src, dst) in enumerate(((v_ref, z_ref), (x1_ref, g1_ref), (x2_ref, g2_ref))):
        _stage_padded(src, pad_ref)

        def conv(c, _, idx=idx, dst=dst):
            r0 = pl.multiple_of(c * chunk, chunk)
            y = _conv_chunk(pad_ref, r0, chunk, (-1, 0, 1), cw_ref.at[idx], cb_ref.at[idx])
            dst[pl.ds(r0, chunk), :] = y.astype(dst.dtype)
            return 0

        lax.fori_loop(0, S // chunk, conv, 0)

    for order, (gate_ref, dst) in enumerate(((g1_ref, z_ref), (g2_ref, o_ref))):
        _stage_a_fwd(z_ref, ma_ref, y_ref, FFT_N1 // 2)

        def slab(k1, _, order=order):
            x = _stage_b_fwd_slab(gb_ref, y_ref, k1)
            kf = kf_ref[order, k1].astype(F32)
            xr, xi = x[:FFT_N2], x[FFT_N2:]
            kr, ki = kf[:FFT_N2], kf[FFT_N2:]
            p = jnp.concatenate([xr * kr - xi * ki, xr * ki + xi * kr], axis=0).astype(BF16)
            y_ref[k1] = jnp.dot(gbi_ref[k1], p, preferred_element_type=F32).astype(y_ref.dtype)
            return 0

        lax.fori_loop(0, FFT_KH, slab, 0, unroll=11)

        bias = fb_ref[order:order + 1, :]
        for p in range(FFT_N2 // g):
            for n1, y in enumerate(_stage_a_inv(y_ref, mai_ref, p)):
                t0 = FFT_N2 * n1 + g * p
                zt = z_ref[t0:t0 + g, :].astype(F32)
                gt = gate_ref[t0:t0 + g, :].astype(F32)
                dst[t0:t0 + g, :] = (gt * (y + bias * zt)).astype(dst.dtype)


def _hyena(proj3, col0, conv_w, conv_b, filt_bias, kf, ma32, gb, gbi, mai, wt=256, chunk=512):
    B, S, _ = proj3.shape
    W = filt_bias.shape[1]
    nj = W // wt
    c0 = col0 // wt
    cw = conv_w.reshape(conv_w.shape[0], 3, W).transpose(1, 0, 2)
    cb = conv_b.reshape(3, 1, W)
    const = lambda *shape: pl.BlockSpec(shape, lambda j, b: (0,) * len(shape), pipeline_mode=pl.Buffered(1))
    return pl.pallas_call(
        functools.partial(_hyena_kernel, chunk=chunk),
        grid=(nj, B),
        in_specs=[
            pl.BlockSpec((None, S, wt), lambda j, b: (b, 0, c0 + j)),
            pl.BlockSpec((None, S, wt), lambda j, b: (b, 0, c0 + nj + j)),
            pl.BlockSpec((None, S, wt), lambda j, b: (b, 0, c0 + 2 * nj + j)),
            pl.BlockSpec((3, conv_w.shape[0], wt), lambda j, b: (0, 0, j)),
            pl.BlockSpec((3, 1, wt), lambda j, b: (0, 0, j)),
            pl.BlockSpec((filt_bias.shape[0], wt), lambda j, b: (0, j)),
            pl.BlockSpec((kf.shape[0], None, FFT_KH, 2 * FFT_N2, wt), lambda j, b: (0, j, 0, 0, 0),
                         pipeline_mode=pl.Buffered(1)),
            const(*ma32.shape), const(*gb.shape), const(*gbi.shape), const(*mai.shape),
        ],
        out_specs=pl.BlockSpec((None, S, wt), lambda j, b: (b, 0, j)),
        out_shape=jax.ShapeDtypeStruct((B, S, W), BF16),
        scratch_shapes=[pltpu.VMEM((S + 2 * PAD, wt), BF16)] + [pltpu.VMEM((S, wt), BF16)] * 3
        + [pltpu.VMEM((FFT_KH, 2 * FFT_N2, wt), BF16)],
        compiler_params=_params(("parallel", "arbitrary")),
        name="hyena",
    )(proj3, proj3, proj3, cw, cb, filt_bias, kf, ma32, gb, gbi, mai)


def _merge_kernel(ua_ref, zb_ref, ga_ref, gbr_ref, x_ref, wa_ref, wb_ref, wo_ref, g_ref, wrh_ref, wrl_ref,
                  x1_ref, h_ref, lg_ref):
    ya = jnp.dot(ua_ref[...], wa_ref[...], preferred_element_type=F32)
    yb = jnp.dot(zb_ref[...], wb_ref[...], preferred_element_type=F32)
    m = _sigmoid(ga_ref[...].astype(F32)) * ya + _sigmoid(gbr_ref[...].astype(F32)) * yb
    x1 = x_ref[...] + jnp.dot(m.astype(BF16), wo_ref[...], preferred_element_type=F32)
    x1_ref[...] = x1
    h = x1 * lax.rsqrt(jnp.mean(x1 * x1, axis=-1, keepdims=True) + EPS) * g_ref[...]
    hh = h.astype(BF16)
    h_ref[...] = hh
    hl = (h - hh.astype(F32)).astype(BF16)
    lg_ref[...] = (jnp.dot(hh, wrh_ref[...], preferred_element_type=F32)
                   + jnp.dot(hl, wrh_ref[...], preferred_element_type=F32)
                   + jnp.dot(hh, wrl_ref[...], preferred_element_type=F32))


def _merge(ua2, zb2, proj2, gate_col0, x2, wa, wb, wo, g_ffn, w_router, tm=512, epad=128):
    T, D = x2.shape
    gc = gate_col0 // D
    E = w_router.shape[1]
    wr = jnp.pad(w_router, ((0, 0), (0, epad - E)))
    wrh = wr.astype(BF16)
    wrl = (wr - wrh.astype(F32)).astype(BF16)
    row = lambda i: (i, 0)
    const = lambda *shape: pl.BlockSpec(shape, lambda i: (0,) * len(shape))
    return pl.pallas_call(
        _merge_kernel,
        grid=(T // tm,),
        in_specs=[
            pl.BlockSpec((tm, D), row), pl.BlockSpec((tm, D), row),
            pl.BlockSpec((tm, D), lambda i: (i, gc)), pl.BlockSpec((tm, D), lambda i: (i, gc + 1)),
            pl.BlockSpec((tm, D), row),
            const(D, D), const(D, D), const(D, D), const(1, D), const(D, epad), const(D, epad),
        ],
        out_specs=[pl.BlockSpec((tm, D), row), pl.BlockSpec((tm, D), row), pl.BlockSpec((tm, epad), row)],
        out_shape=[jax.ShapeDtypeStruct((T, D), F32), jax.ShapeDtypeStruct((T, D), BF16),
                   jax.ShapeDtypeStruct((T, epad), F32)],
        compiler_params=_params(("parallel",)),
        name="merge",
    )(ua2, zb2, proj2, proj2, x2, wa, wb, wo, g_ffn.reshape(1, D), wrh, wrl)


def _route_kernel(lg_ref, tri_ref, lmat_ref, affrow_ref, srow_ref, scol_ref, *, n_exp, cap):
    S, EP = lg_ref.shape
    nchunk = S // 128
    lane = lax.broadcasted_iota(jnp.int32, (S, EP), 1)
    lg = jnp.where(lane < n_exp, lg_ref[...], -1e30)
    e = jnp.exp(lg - jnp.max(lg, axis=-1, keepdims=True))
    aff = e / jnp.sum(e, axis=-1, keepdims=True)
    rows = [aff[c * 128:(c + 1) * 128, :].T[:n_exp, :] for c in range(nchunk)]
    ar = jnp.concatenate(rows, axis=1)
    affrow_ref[...] = ar

    def count_ge(thr):
        return jnp.sum(jnp.where(ar >= thr, 1.0, 0.0), axis=-1, keepdims=True)

    def cond(state):
        lo, hi, it = state
        mid = 0.5 * (lo + hi)
        open_ = jnp.logical_and(mid != lo, mid != hi)
        return jnp.logical_and(jnp.max(jnp.where(open_, 1.0, 0.0)) > 0.0, it < 400)

    def body(state):
        lo, hi, it = state
        mid = 0.5 * (lo + hi)
        ok = count_ge(mid) >= cap
        return jnp.where(ok, mid, lo), jnp.where(ok, hi, mid), it + 1

    lo0 = jnp.zeros((n_exp, 1), F32)
    hi0 = jnp.full((n_exp, 1), 2.0, F32)
    thr, _, _ = lax.while_loop(cond, body, (lo0, hi0, jnp.int32(0)))

    def cumsum_excl(x):
        xs = jnp.concatenate([x[:, c * 128:(c + 1) * 128] for c in range(nchunk)], axis=0)
        cs = jnp.dot(xs.astype(BF16), tri_ref[...], preferred_element_type=F32)
        off = jnp.dot(lmat_ref[...], cs.astype(BF16), preferred_element_type=F32)[:, 127:128]
        inc = cs + off
        return jnp.concatenate([inc[c * n_exp:(c + 1) * n_exp, :] for c in range(nchunk)], axis=1) - x

    gt = ar > thr
    eq = ar == thr
    need = cap - jnp.sum(jnp.where(gt, 1.0, 0.0), axis=-1, keepdims=True)
    tie_rank = cumsum_excl(jnp.where(eq, 1.0, 0.0))
    sel = jnp.logical_or(gt, jnp.logical_and(eq, tie_rank < need))
    self_ = jnp.where(sel, 1.0, 0.0)
    slot = jnp.where(sel, cumsum_excl(self_), -1.0)
    srow_ref[...] = slot.astype(jnp.int32)
    slot_p = jnp.concatenate([slot, jnp.full((EP - n_exp, S), -1.0, F32)], axis=0)
    cols = [slot_p[:, c * 128:(c + 1) * 128].T for c in range(nchunk)]
    scol_ref[...] = jnp.concatenate(cols, axis=0).astype(jnp.int32)


def _route(logits3, n_exp, cap):
    B, S, EP = logits3.shape
    nchunk = S // 128
    i = np.arange(128)
    tri = (i[:, None] <= i[None, :]).astype(np.float32)
    r = np.arange(nchunk * n_exp)
    lmat = ((r[:, None] % n_exp == r[None, :] % n_exp) & (r[None, :] // n_exp < r[:, None] // n_exp)).astype(np.float32)
    const = lambda *shape: pl.BlockSpec(shape, lambda b: (0,) * len(shape))
    return pl.pallas_call(
        functools.partial(_route_kernel, n_exp=n_exp, cap=cap),
        grid=(B,),
        in_specs=[pl.BlockSpec((None, S, EP), lambda b: (b, 0, 0)), const(128, 128), const(*lmat.shape)],
        out_specs=[
            pl.BlockSpec((None, n_exp, S), lambda b: (b, 0, 0)),
            pl.BlockSpec((None, n_exp, S), lambda b: (b, 0, 0)),
            pl.BlockSpec((None, S, EP), lambda b: (b, 0, 0)),
        ],
        out_shape=[
            jax.ShapeDtypeStruct((B, n_exp, S), F32),
            jax.ShapeDtypeStruct((B, n_exp, S), jnp.int32), jax.ShapeDtypeStruct((B, S, EP), jnp.int32),
        ],
        compiler_params=_params(("parallel",)),
        name="route",
    )(logits3, jnp.asarray(tri, BF16), jnp.asarray(lmat, BF16))


def _gather_kernel(h_ref, srow_ref, affrow_ref, xe_ref, val_ref, *, cap):
    e = pl.program_id(1)
    S = h_ref.shape[0]
    slot = srow_ref[pl.ds(e, 1), :]
    hit = lax.broadcasted_iota(jnp.int32, (cap, S), 0) == slot
    xe_ref[...] = jnp.dot(jnp.where(hit, 1.0, 0.0).astype(BF16), h_ref[...],
                          preferred_element_type=F32).astype(xe_ref.dtype)
    val_ref[...] = jnp.sum(jnp.where(hit, affrow_ref[pl.ds(e, 1), :], 0.0), axis=-1, keepdims=True)


def _gather(h3, srow, affrow, cap):
    B, S, D = h3.shape
    E = srow.shape[1]
    return pl.pallas_call(
        functools.partial(_gather_kernel, cap=cap),
        grid=(B, E),
        in_specs=[
            pl.BlockSpec((None, S, D), lambda b, e: (b, 0, 0)),
            pl.BlockSpec((None, E, S), lambda b, e: (b, 0, 0)),
            pl.BlockSpec((None, E, S), lambda b, e: (b, 0, 0)),
        ],
        out_specs=[
            pl.BlockSpec((None, None, cap, D), lambda b, e: (b, e, 0, 0)),
            pl.BlockSpec((None, None, cap, 1), lambda b, e: (b, e, 0, 0)),
        ],
        out_shape=[jax.ShapeDtypeStruct((B, E, cap, D), BF16), jax.ShapeDtypeStruct((B, E, cap, 1), F32)],
        compiler_params=_params(("parallel", "arbitrary")),
        name="gather",
    )(h3, srow, affrow)


def _ffn_kernel(xe_ref, val_ref, wg_ref, wu_ref, wd_ref, ye_ref):
    xe = xe_ref[...]
    gt = jnp.dot(xe, wg_ref[...], preferred_element_type=F32)
    up = jnp.dot(xe, wu_ref[...], preferred_element_type=F32)
    act = (gt * _sigmoid(gt) * up).astype(BF16)
    ye = jnp.dot(act, wd_ref[...], preferred_element_type=F32)
    ye_ref[...] = (ye * val_ref[...]).astype(ye_ref.dtype)


def _ffn(xe, vals, wg, wu, wd):
    B, E, C, D = xe.shape
    F = wg.shape[2]
    return pl.pallas_call(
        _ffn_kernel,
        grid=(E, B),
        in_specs=[
            pl.BlockSpec((None, None, C, D), lambda e, b: (b, e, 0, 0)),
            pl.BlockSpec((None, None, C, 1), lambda e, b: (b, e, 0, 0)),
            pl.BlockSpec((None, D, F), lambda e, b: (e, 0, 0)),
            pl.BlockSpec((None, D, F), lambda e, b: (e, 0, 0)),
            pl.BlockSpec((None, F, D), lambda e, b: (e, 0, 0)),
        ],
        out_specs=pl.BlockSpec((None, None, C, D), lambda e, b: (b, e, 0, 0)),
        out_shape=jax.ShapeDtypeStruct((B, E, C, D), BF16),
        compiler_params=_params(("parallel", "arbitrary")),
        name="ffn",
    )(xe, vals, wg, wu, wd)


def _combine_kernel(ye_ref, scol_ref, x1_ref, g_ref, o_ref, *, n_exp, cap):
    tm = x1_ref.shape[0]
    lane = lax.broadcasted_iota(jnp.int32, (tm, cap), 1)
    scol = scol_ref[...]
    acc = x1_ref[...]
    for e in range(n_exp):
        hit = jnp.where(lane == scol[:, e:e + 1], 1.0, 0.0).astype(BF16)
        acc = acc + jnp.dot(hit, ye_ref[e], preferred_element_type=F32)
    y = acc * lax.rsqrt(jnp.mean(acc * acc, axis=-1, keepdims=True) + EPS)
    o_ref[...] = y * g_ref[...]


def _combine(ye, scol, x13, g_final, tm=512):
    B, E, C, D = ye.shape
    S = x13.shape[1]
    EP = scol.shape[2]
    return pl.pallas_call(
        functools.partial(_combine_kernel, n_exp=E, cap=C),
        grid=(B, S // tm),
        in_specs=[
            pl.BlockSpec((None, E, C, D), lambda b, r: (b, 0, 0, 0)),
            pl.BlockSpec((None, tm, EP), lambda b, r: (b, r, 0)),
            pl.BlockSpec((None, tm, D), lambda b, r: (b, r, 0)),
            pl.BlockSpec((1, D), lambda b, r: (0, 0)),
        ],
        out_specs=pl.BlockSpec((None, tm, D), lambda b, r: (b, r, 0)),
        out_shape=jax.ShapeDtypeStruct((B, S, D), F32),
        compiler_params=_params(("parallel", "arbitrary")),
        name="combine",
    )(ye, scol, x13, g_final.reshape(1, D))


def kernel(x, g_mix, w_in, conv_a_w, conv_a_b, lru_w_r, lru_b_r, lru_w_i, lru_b_i, lru_lambda, w_a_out, conv_b_w, conv_b_b, filt_w1, filt_b1, filt_w2, filt_b2, filt_w3, filt_freq, filt_bias, w_b_out, w_o, g_ffn, w_router, w_gate, w_up, w_down, g_final):
    B, S, D = x.shape
    assert w_in.shape[0] == 1, "single-layer block only"
    l = 0
    lru_w = conv_a_w.shape[2]
    hy_w = filt_bias.shape[2]
    n_exp = w_router.shape[2]
    cap = CAPACITY_FACTOR * S // n_exp
    assert 2 * S == N_FFT
    ma32, ma64, gb, gbi, mai = (jnp.asarray(c).astype(BF16) for c in _dft_constants())

    x2 = x.reshape(B * S, D)
    proj2 = _inproj(x2, g_mix[l], w_in[l].astype(BF16))
    proj3 = proj2.reshape(B, S, -1)
    ua = _rglru(proj3, conv_a_w[l], conv_a_b[l], lru_w_r[l], lru_b_r[l], lru_w_i[l], lru_b_i[l], lru_lambda[l])
    kf = _filter_spectra(S, filt_w1[l], filt_b1[l], filt_w2[l], filt_b2[l], filt_w3[l], filt_freq[l], hy_w, ma64, gb)
    zb = _hyena(proj3, 2 * lru_w, conv_b_w[l], conv_b_b[l], filt_bias[l], kf, ma32, gb, gbi, mai)
    x1, h, logits = _merge(ua.reshape(B * S, lru_w), zb.reshape(B * S, hy_w), proj2, 2 * lru_w + 3 * hy_w, x2,
                           w_a_out[l].astype(BF16), w_b_out[l].astype(BF16), w_o[l].astype(BF16), g_ffn[l],
                           w_router[l])
    affrow, srow, scol = _route(logits.reshape(B, S, -1), n_exp, cap)
    xe, vals = _gather(h.reshape(B, S, D), srow, affrow, cap)
    ye = _ffn(xe, vals, w_gate[l].astype(BF16), w_up[l].astype(BF16), w_down[l].astype(BF16))
    return _combine(ye, scol, x1.reshape(B, S, D), g_final)
```

```python
import functools
import math

import numpy as np
import jax
import jax.numpy as jnp
from jax import lax
from jax.experimental import pallas as pl
from jax.experimental.pallas import tpu as pltpu

F32 = jnp.float32
BF16 = jnp.bfloat16
EPS = 1e-6
LOG2E = 1.4426950408889634

VMEM_LIMIT_BYTES = 56 * 1024 * 1024
BF16_ROWS = 16
F32_ROWS = 8
LANES = 128

LRU_HEADS = 16
LRU_C = 8.0
CAPACITY_FACTOR = 2
FILTER_BANDS = 16
DECAY_FAST = 0.3
DECAY_SLOW = 1.5
DECAY_TARGET = 1e-2

FFT_N1 = 64
FFT_N2 = 128
N_FFT = FFT_N1 * FFT_N2
FFT_KH = FFT_N1 // 2 + 1
FFT_G = F32_ROWS


def _params(sem):
    return pltpu.CompilerParams(dimension_semantics=sem, vmem_limit_bytes=VMEM_LIMIT_BYTES)


def _inproj_kernel(x_ref, g_ref, w_ref, o_ref, h_ref):
    @pl.when(pl.program_id(1) == 0)
    def _():
        x = x_ref[...]
        y = x * lax.rsqrt(jnp.mean(x * x, axis=-1, keepdims=True) + EPS)
        h_ref[...] = (y * g_ref[...]).astype(BF16)

    o_ref[...] = jnp.dot(h_ref[...], w_ref[...].astype(BF16), preferred_element_type=F32).astype(o_ref.dtype)


def _inproj(x2, g, w, tm=2048, tn=1024):
    T, D = x2.shape
    N = w.shape[1]
    return pl.pallas_call(
        _inproj_kernel,
        grid=(T // tm, N // tn),
        in_specs=[
            pl.BlockSpec((tm, D), lambda i, j: (i, 0)),
            pl.BlockSpec((1, D), lambda i, j: (0, 0)),
            pl.BlockSpec((D, tn), lambda i, j: (0, j)),
        ],
        out_specs=pl.BlockSpec((tm, tn), lambda i, j: (i, j)),
        out_shape=jax.ShapeDtypeStruct((T, N), BF16),
        scratch_shapes=[pltpu.VMEM((tm, D), BF16)],
        compiler_params=_params(("parallel", "arbitrary")),
        name="inproj",
    )(x2, g.reshape(1, D), w)


PAD = BF16_ROWS


def _stage_padded(src_ref, pad_ref):
    S, W = src_ref.shape
    pad_ref[0:PAD, :] = jnp.zeros((PAD, W), pad_ref.dtype)
    pad_ref[PAD + S:PAD + S + PAD, :] = jnp.zeros((PAD, W), pad_ref.dtype)
    pad_ref[PAD:PAD + S, :] = src_ref[...]


def _conv_chunk(pad_ref, r0, rows, taps, w_ref, b_ref):
    n = rows + 2 * PAD
    win = pad_ref[pl.ds(r0, n), :].astype(F32)
    acc = None
    for k, off in enumerate(taps):
        sh = win if off == 0 else pltpu.roll(win, (-off) % n, 0)
        term = sh[PAD:PAD + rows, :] * w_ref[k:k + 1, :]
        acc = term if acc is None else acc + term
    return acc + b_ref[...]


NSEG = F32_ROWS
SCAN_UNROLL = 8


def _sigmoid(x):
    return 1.0 / (1.0 + jnp.exp2(x * (-LOG2E)))


def _gelu_tanh(x):
    return 0.5 * x * (1.0 + jnp.tanh(math.sqrt(2.0 / math.pi) * (x + 0.044715 * (x * x * x))))


def _softplus(x):
    return jnp.maximum(x, 0.0) + jnp.log(1.0 + jnp.exp(-jnp.abs(x)))


def _ld(ref, rows):
    return jnp.concatenate([ref[l, rows, :] for l in range(ref.shape[0])], axis=1)


def _st(ref, rows, v):
    for l in range(ref.shape[0]):
        ref[l, rows, :] = v[:, l * LANES:(l + 1) * LANES]


def _rglru_kernel(ax_ref, gate_ref, cw_ref, cb_ref, wcat_ref, bias_ref, lam_ref, o_ref,
                  pad_ref, af_ref, bf_ref, ab_ref, bb_ref, hs_ref):
    S, W = ax_ref.shape
    seg = S // NSEG
    pitch = hs_ref.shape[1] // NSEG
    _stage_padded(ax_ref, pad_ref)
    coef = [(LRU_C * LOG2E) * _softplus(-lam_ref[d:d + 1, :]) for d in range(2)]

    def gates(c, _):
        r0 = pl.multiple_of(c * seg, seg)
        xa = _conv_chunk(pad_ref, r0, seg, (-2, -1, 0, 1), cw_ref, cb_ref)
        pre = jnp.dot(xa.astype(BF16), wcat_ref[...], preferred_element_type=F32)
        pos = r0 + lax.broadcasted_iota(jnp.int32, (seg, W), 0)
        for d, (a_ref, b_ref, start) in enumerate(((af_ref, bf_ref, 0), (ab_ref, bb_ref, S - 1))):
            r = _sigmoid(pre[:, (2 * d) * W:(2 * d + 1) * W] + bias_ref[2 * d:2 * d + 1, :])
            i = _sigmoid(pre[:, (2 * d + 1) * W:(2 * d + 2) * W] + bias_ref[2 * d + 1:2 * d + 2, :])
            a = jnp.exp2(-coef[d] * r)
            m2 = 1.0 - a * a
            mult = jnp.where(m2 > 0.0, m2 * lax.rsqrt(m2), 0.0)
            mult = jnp.where(pos == start, 1.0, mult)
            _st(a_ref, pl.ds(c, seg, stride=NSEG), a)
            _st(b_ref, pl.ds(c, seg, stride=NSEG), mult * (i * xa))
        return 0

    lax.fori_loop(0, NSEG, gates, 0)

    def scan(u, carry):
        hf, pf, hb, pb = carry
        for k in range(SCAN_UNROLL):
            j = u * SCAN_UNROLL + k
            rf = pl.multiple_of(j * NSEG, NSEG)
            rb = pl.multiple_of((seg - 1 - j) * NSEG, NSEG)
            a = _ld(af_ref, pl.ds(rf, NSEG))
            hf = a * hf + _ld(bf_ref, pl.ds(rf, NSEG))
            pf = a * pf
            _st(bf_ref, pl.ds(rf, NSEG), hf)
            _st(af_ref, pl.ds(rf, NSEG), pf)
            a = _ld(ab_ref, pl.ds(rb, NSEG))
            hb = a * hb + _ld(bb_ref, pl.ds(rb, NSEG))
            pb = a * pb
            _st(bb_ref, pl.ds(rb, NSEG), hb)
            _st(ab_ref, pl.ds(rb, NSEG), pb)
        return hf, pf, hb, pb

    zero = jnp.zeros((NSEG, W), F32)
    one = jnp.ones((NSEG, W), F32)
    hf, pf, hb, pb = lax.fori_loop(0, seg // SCAN_UNROLL, scan, (zero, one, zero, one))

    row = lax.broadcasted_iota(jnp.int32, (NSEG, W), 0)
    cf = zero
    cb = zero
    for _ in range(NSEG - 1):
        cf = jnp.where(row >= 1, pltpu.roll(hf + pf * cf, 1, 0), 0.0)
        cb = jnp.where(row < NSEG - 1, pltpu.roll(hb + pb * cb, NSEG - 1, 0), 0.0)

    def fix(u, _):
        for k in range(SCAN_UNROLL):
            j = u * SCAN_UNROLL + k
            r = pl.multiple_of(j * NSEG, NSEG)
            rows = pl.ds(r, NSEG)
            h = _ld(bf_ref, rows) + _ld(af_ref, rows) * cf + _ld(bb_ref, rows) + _ld(ab_ref, rows) * cb
            _st(hs_ref, pl.ds(j, NSEG, stride=pitch), h)
        return 0

    lax.fori_loop(0, seg // SCAN_UNROLL, fix, 0)

    def out(c, _):
        r0 = pl.multiple_of(c * seg, seg)
        h = _ld(hs_ref, pl.ds(pl.multiple_of(c * pitch, F32_ROWS), seg))
        g = gate_ref[pl.ds(r0, seg), :].astype(F32)
        o_ref[pl.ds(r0, seg), :] = (h * _gelu_tanh(g)).astype(o_ref.dtype)
        return 0

    lax.fori_loop(0, NSEG, out, 0)


def _rglru(proj3, conv_w, conv_b, w_r, b_r, w_i, b_i, lam, wt=256):
    B, S, _ = proj3.shape
    W = conv_w.shape[1]
    nj = W // wt
    hd = W // LRU_HEADS
    hpt = wt // hd
    pitch = S // NSEG + F32_ROWS

    def blockdiag(w):
        w4 = w.reshape(nj, hpt, hd, hd)
        eye = jnp.eye(hpt, dtype=w.dtype)
        return jnp.einsum('jhab,hg->jhagb', w4, eye).reshape(nj, wt, wt)

    wcat = jnp.concatenate([blockdiag(w_r[0]), blockdiag(w_i[0]), blockdiag(w_r[1]), blockdiag(w_i[1])],
                           axis=-1).astype(BF16)
    bias = jnp.stack([b_r[0], b_i[0], b_r[1], b_i[1]], axis=0)
    gate_off = W // wt
    return pl.pallas_call(
        _rglru_kernel,
        grid=(B, nj),
        in_specs=[
            pl.BlockSpec((None, S, wt), lambda b, j: (b, 0, j)),
            pl.BlockSpec((None, S, wt), lambda b, j: (b, 0, gate_off + j)),
            pl.BlockSpec((conv_w.shape[0], wt), lambda b, j: (0, j)),
            pl.BlockSpec((1, wt), lambda b, j: (0, j)),
            pl.BlockSpec((None, wt, 4 * wt), lambda b, j: (j, 0, 0)),
            pl.BlockSpec((4, wt), lambda b, j: (0, j)),
            pl.BlockSpec((2, wt), lambda b, j: (0, j)),
        ],
        out_specs=pl.BlockSpec((None, S, wt), lambda b, j: (b, 0, j)),
        out_shape=jax.ShapeDtypeStruct((B, S, W), BF16),
        scratch_shapes=[pltpu.VMEM((S + 2 * PAD, wt), BF16)] + [pltpu.VMEM((wt // LANES, S, LANES), F32)] * 4
        + [pltpu.VMEM((wt // LANES, NSEG * pitch, LANES), F32)],
        compiler_params=_params(("parallel", "parallel")),
        name="rglru",
    )(proj3, proj3, conv_w, conv_b.reshape(1, W), wcat, bias, lam)


COMPS_FWD = [(c, k1) for c in range(2) for k1 in range(FFT_KH)]
COMPS_INV = [(c, k1) for (c, k1) in COMPS_FWD if not (c == 1 and k1 in (0, FFT_KH - 1))]


@functools.lru_cache(maxsize=None)
def _dft_constants():
    g = FFT_G
    eye = np.eye(g)

    def stage_a(n1_in):
        k1 = np.arange(FFT_KH)[:, None]
        n1 = np.arange(n1_in)[None, :]
        ang = -2.0 * np.pi * ((k1 * n1) % FFT_N1) / FFT_N1
        a = np.stack([np.cos(ang), np.sin(ang)])
        a = np.stack([a[c, k] for (c, k) in COMPS_FWD])
        return np.einsum('qn,st->qsnt', a, eye).reshape(len(COMPS_FWD) * g, n1_in * g)

    k1 = np.arange(FFT_KH)[:, None, None]
    k2 = np.arange(FFT_N2)[None, :, None]
    n2 = np.arange(FFT_N2)[None, None, :]
    ang = -2.0 * np.pi * ((n2 * (k1 + FFT_N1 * k2)) % N_FFT) / N_FFT
    gr, gi = np.cos(ang), np.sin(ang)
    gb = np.concatenate([np.concatenate([gr, -gi], axis=2), np.concatenate([gi, gr], axis=2)], axis=1)
    gbi = np.transpose(gb, (0, 2, 1))

    c = np.full(FFT_KH, 2.0)
    c[0] = 1.0
    c[-1] = 1.0
    n1 = np.arange(FFT_N1 // 2)[:, None]
    kk = np.arange(FFT_KH)[None, :]
    ang = 2.0 * np.pi * ((n1 * kk) % FFT_N1) / FFT_N1
    bi = np.stack([np.cos(ang) * c / N_FFT, -np.sin(ang) * c / N_FFT])
    bi = np.stack([bi[c_, :, k] for (c_, k) in COMPS_INV], axis=1)
    mai = np.einsum('nq,st->nsqt', bi, eye).reshape((FFT_N1 // 2) * g, len(COMPS_INV) * g)
    f = np.float32
    return stage_a(FFT_N1 // 2).astype(f), stage_a(FFT_N1).astype(f), gb.astype(f), gbi.astype(f), mai.astype(f)


def _split_rows(tiles):
    f = [t.astype(F32) for t in tiles]
    return [jnp.concatenate([t[h * FFT_G:(h + 1) * FFT_G] for t in f], axis=0).astype(BF16)
            for h in range(BF16_ROWS // FFT_G)]


def _join_rows(outs, q):
    return jnp.concatenate([o[q * FFT_G:(q + 1) * FFT_G] for o in outs], axis=0)


def _stage_a_fwd(src_ref, ma_ref, y_ref, n1_in):
    for p in range(FFT_N2 // BF16_ROWS):
        r = BF16_ROWS * p
        opnds = _split_rows([src_ref[FFT_N2 * n1 + r:FFT_N2 * n1 + r + BF16_ROWS, :] for n1 in range(n1_in)])
        outs = [jnp.dot(ma_ref[...], o, preferred_element_type=F32) for o in opnds]
        for q, (c, k1) in enumerate(COMPS_FWD):
            y_ref[k1, c * FFT_N2 + r:c * FFT_N2 + r + BF16_ROWS, :] = _join_rows(outs, q).astype(y_ref.dtype)


def _stage_a_inv(y_ref, mai_ref, p):
    r = BF16_ROWS * p
    opnds = _split_rows([y_ref[k1, c * FFT_N2 + r:c * FFT_N2 + r + BF16_ROWS, :] for (c, k1) in COMPS_INV])
    outs = [jnp.dot(mai_ref[...], o, preferred_element_type=F32) for o in opnds]
    return [_join_rows(outs, n1) for n1 in range(FFT_N1 // 2)]


def _stage_b_fwd_slab(gb_ref, y_ref, k1):
    return jnp.dot(gb_ref[k1], y_ref[k1], preferred_element_type=F32)


def _dot3(a, b):
    ah = a.astype(BF16)
    al = (a - ah.astype(F32)).astype(BF16)
    bh = b.astype(BF16)
    bl = (b - bh.astype(F32)).astype(BF16)
    d = lambda u, v: jnp.dot(u, v, preferred_element_type=F32)
    return d(ah, bh) + d(al, bh) + d(ah, bl)


def _filtmlp_kernel(fa_ref, fb_ref, w1_ref, b1_ref, w2_ref, b2_ref, freq_ref, o_ref):
    u = jnp.concatenate([_dot3(fa_ref[...], w1_ref[...]), _dot3(fb_ref[...], w1_ref[...])], axis=1)
    h = jnp.sin(freq_ref[0:1, :] * (u + b1_ref[...]))
    o_ref[...] = jnp.sin(freq_ref[1:2, :] * (_dot3(h, w2_ref[...]) + b2_ref[...]))


def _filtfft_kernel(h_ref, t_ref, w3f_ref, w3b_ref, delta_ref, ma_ref, gb_ref, o_ref, kc_ref, y_ref, *, chunk):
    n, W = kc_ref.shape
    L = n // 2

    def body(c, _):
        r0 = pl.multiple_of(c * chunk, chunk)
        h = h_ref[pl.ds(r0, chunk), :].astype(BF16)
        kf = jnp.dot(h, w3f_ref[...], preferred_element_type=F32)
        kb = jnp.dot(h, w3b_ref[...], preferred_element_type=F32)
        row = r0 + lax.broadcasted_iota(jnp.int32, (chunk, W), 0)
        kb = jnp.where(row == 0, 0.0, kb)
        kc_ref[pl.ds(r0, chunk), :] = (kf * jnp.exp(-t_ref[pl.ds(r0, chunk), :] * delta_ref[...])).astype(kc_ref.dtype)
        kc_ref[pl.ds(L + r0, chunk), :] = (kb * jnp.exp(-t_ref[pl.ds(L + r0, chunk), :] * delta_ref[...])).astype(kc_ref.dtype)
        return 0

    lax.fori_loop(0, L // chunk, body, 0)
    _stage_a_fwd(kc_ref, ma_ref, y_ref, FFT_N1)

    def slab(k1, _):
        o_ref[k1] = _stage_b_fwd_slab(gb_ref, y_ref, k1).astype(o_ref.dtype)
        return 0

    lax.fori_loop(0, FFT_KH, slab, 0, unroll=11)


def _filter_spectra(L, w1, b1, w2, b2, w3, freq, W, ma64, gb, wt=256, chunk=512):
    f32 = F32
    order = w3.shape[1] // (2 * W)
    hid = w1.shape[1]
    t = jnp.linspace(0.0, 1.0, L, dtype=f32)[:, None]
    w = (2.0 * math.pi / L) * jnp.arange(L, dtype=f32)[:, None]
    f = jnp.linspace(1e-4, FILTER_BANDS - 1, FILTER_BANDS, dtype=f32)[None, :]
    z = jnp.concatenate([t, jnp.cos(w * f), -jnp.sin(w * f)], axis=-1)
    emb = z.shape[1]
    embp = 128
    z = jnp.pad(z, ((0, 0), (0, embp - emb)))
    feat = jnp.concatenate([z, z[:1], z[:0:-1]], axis=0)
    tt = jnp.concatenate([t, t[:1], t[:0:-1]], axis=0)
    w1p = jnp.pad(w1, ((0, embp - emb), (0, 0)))
    zero = jnp.zeros_like(w2)
    w2d = jnp.concatenate([jnp.concatenate([w2, zero], axis=1), jnp.concatenate([zero, w2], axis=1)], axis=0)
    two = lambda v: jnp.concatenate([v, v], axis=-1)
    nblk = L // chunk
    const1 = lambda *shape: pl.BlockSpec(shape, lambda i: (0,) * len(shape))
    hfeat = pl.pallas_call(
        _filtmlp_kernel,
        grid=(nblk,),
        in_specs=[
            pl.BlockSpec((chunk, embp), lambda i: (i, 0)), pl.BlockSpec((chunk, embp), lambda i: (nblk + i, 0)),
            const1(embp, hid), const1(1, 2 * hid), const1(2 * hid, 2 * hid), const1(1, 2 * hid), const1(2, 2 * hid),
        ],
        out_specs=pl.BlockSpec((chunk, 2 * hid), lambda i: (i, 0)),
        out_shape=jax.ShapeDtypeStruct((L, 2 * hid), f32),
        compiler_params=_params(("parallel",)),
        name="filtmlp",
    )(feat, feat, w1p, two(b1.reshape(1, hid)), w2d, two(b2.reshape(1, hid)), two(freq))

    w3b16 = w3.astype(BF16)
    zero3 = jnp.zeros_like(w3b16)
    w3f = jnp.concatenate([w3b16, zero3], axis=0)
    w3b = jnp.concatenate([zero3, w3b16], axis=0)
    min_decay = math.log(DECAY_TARGET) / DECAY_SLOW
    max_decay = math.log(DECAY_TARGET) / DECAY_FAST
    deltas = jnp.abs(jnp.linspace(min_decay, max_decay, W, dtype=f32))[None, :]
    nj = W // wt
    n = 2 * L
    const = lambda *shape: pl.BlockSpec(shape, lambda o, j: (0,) * len(shape))
    return pl.pallas_call(
        functools.partial(_filtfft_kernel, chunk=chunk),
        grid=(order, nj),
        in_specs=[
            const(L, 2 * hid), const(n, 1),
            pl.BlockSpec((2 * hid, wt), lambda o, j: (0, (2 * o) * nj + j)),
            pl.BlockSpec((2 * hid, wt), lambda o, j: (0, (2 * o + 1) * nj + j)),
            pl.BlockSpec((1, wt), lambda o, j: (0, j)),
            const(*ma64.shape), const(*gb.shape),
        ],
        out_specs=pl.BlockSpec((None, None, FFT_KH, 2 * FFT_N2, wt), lambda o, j: (o, j, 0, 0, 0)),
        out_shape=jax.ShapeDtypeStruct((order, nj, FFT_KH, 2 * FFT_N2, wt), BF16),
        scratch_shapes=[pltpu.VMEM((n, wt), BF16), pltpu.VMEM((FFT_KH, 2 * FFT_N2, wt), BF16)],
        compiler_params=_params(("parallel", "parallel")),
        name="filtfft",
    )(hfeat, tt, w3f, w3b, deltas, ma64, gb)


def _hyena_kernel(v_ref, x1_ref, x2_ref, cw_ref, cb_ref, fb_ref, kf_ref, ma_ref, gb_ref, gbi_ref, mai_ref, o_ref,
                  pad_ref, z_ref, g1_ref, g2_ref, y_ref, *, chunk):
    S, W = v_ref.shape
    g = BF16_ROWS

    for idx, (src, dst) in enumerate(((v_ref, z_ref), (x1_ref, g1_ref), (x2_ref, g2_ref))):
        _stage_padded(src, pad_ref)

        def conv(c, _, idx=idx, dst=dst):
            r0 = pl.multiple_of(c * chunk, chunk)
            y = _conv_chunk(pad_ref, r0, chunk, (-1, 0, 1), cw_ref.at[idx], cb_ref.at[idx])
            dst[pl.ds(r0, chunk), :] = y.astype(dst.dtype)
            return 0

        lax.fori_loop(0, S // chunk, conv, 0)

    for order, (gate_ref, dst) in enumerate(((g1_ref, z_ref), (g2_ref, o_ref))):
        _stage_a_fwd(z_ref, ma_ref, y_ref, FFT_N1 // 2)

        def slab(k1, _, order=order):
            x = _stage_b_fwd_slab(gb_ref, y_ref, k1)
            kf = kf_ref[order, k1].astype(F32)
            xr, xi = x[:FFT_N2], x[FFT_N2:]
            kr, ki = kf[:FFT_N2], kf[FFT_N2:]
            p = jnp.concatenate([xr * kr - xi * ki, xr * ki + xi * kr], axis=0).astype(BF16)
            y_ref[k1] = jnp.dot(gbi_ref[k1], p, preferred_element_type=F32).astype(y_ref.dtype)
            return 0

        lax.fori_loop(0, FFT_KH, slab, 0, unroll=11)

        bias = fb_ref[order:order + 1, :]
        for p in range(FFT_N2 // g):
            for n1, y in enumerate(_stage_a_inv(y_ref, mai_ref, p)):
                t0 = FFT_N2 * n1 + g * p
                zt = z_ref[t0:t0 + g, :].astype(F32)
                gt = gate_ref[t0:t0 + g, :].astype(F32)
                dst[t0:t0 + g, :] = (gt * (y + bias * zt)).astype(dst.dtype)


def _hyena(proj3, col0, conv_w, conv_b, filt_bias, kf, ma32, gb, gbi, mai, wt=256, chunk=512):
    B, S, _ = proj3.shape
    W = filt_bias.shape[1]
    nj = W // wt
    c0 = col0 // wt
    cw = conv_w.reshape(conv_w.shape[0], 3, W).transpose(1, 0, 2)
    cb = conv_b.reshape(3, 1, W)
    const = lambda *shape: pl.BlockSpec(shape, lambda j, b: (0,) * len(shape), pipeline_mode=pl.Buffered(1))
    return pl.pallas_call(
        functools.partial(_hyena_kernel, chunk=chunk),
        grid=(nj, B),
        in_specs=[
            pl.BlockSpec((None, S, wt), lambda j, b: (b, 0, c0 + j)),
            pl.BlockSpec((None, S, wt), lambda j, b: (b, 0, c0 + nj + j)),
            pl.BlockSpec((None, S, wt), lambda j, b: (b, 0, c0 + 2 * nj + j)),
            pl.BlockSpec((3, conv_w.shape[0], wt), lambda j, b: (0, 0, j)),
            pl.BlockSpec((3, 1, wt), lambda j, b: (0, 0, j)),
            pl.BlockSpec((filt_bias.shape[0], wt), lambda j, b: (0, j)),
            pl.BlockSpec((kf.shape[0], None, FFT_KH, 2 * FFT_N2, wt), lambda j, b: (0, j, 0, 0, 0),
                         pipeline_mode=pl.Buffered(1)),
            const(*ma32.shape), const(*gb.shape), const(*gbi.shape), const(*mai.shape),
        ],
        out_specs=pl.BlockSpec((None, S, wt), lambda j, b: (b, 0, j)),
        out_shape=jax.ShapeDtypeStruct((B, S, W), BF16),
        scratch_shapes=[pltpu.VMEM((S + 2 * PAD, wt), BF16)] + [pltpu.VMEM((S, wt), BF16)] * 3
        + [pltpu.VMEM((FFT_KH, 2 * FFT_N2, wt), BF16)],
        compiler_params=_params(("parallel", "arbitrary")),
        name="hyena",
    )(proj3, proj3, proj3, cw, cb, filt_bias, kf, ma32, gb, gbi, mai)


def _merge_kernel(ua_ref, zb_ref, ga_ref, gbr_ref, x_ref, wa_ref, wb_ref, wo_ref, g_ref, wrh_ref, wrl_ref,
                  x1_ref, h_ref, lg_ref):
    ya = jnp.dot(ua_ref[...], wa_ref[...], preferred_element_type=F32)
    yb = jnp.dot(zb_ref[...], wb_ref[...], preferred_element_type=F32)
    m = _sigmoid(ga_ref[...].astype(F32)) * ya + _sigmoid(gbr_ref[...].astype(F32)) * yb
    x1 = x_ref[...] + jnp.dot(m.astype(BF16), wo_ref[...], preferred_element_type=F32)
    x1_ref[...] = x1
    h = x1 * lax.rsqrt(jnp.mean(x1 * x1, axis=-1, keepdims=True) + EPS) * g_ref[...]
    hh = h.astype(BF16)
    h_ref[...] = hh
    hl = (h - hh.astype(F32)).astype(BF16)
    lg_ref[...] = (jnp.dot(hh, wrh_ref[...], preferred_element_type=F32)
                   + jnp.dot(hl, wrh_ref[...], preferred_element_type=F32)
                   + jnp.dot(hh, wrl_ref[...], preferred_element_type=F32))


def _merge(ua2, zb2, proj2, gate_col0, x2, wa, wb, wo, g_ffn, w_router, tm=512, epad=128):
    T, D = x2.shape
    gc = gate_col0 // D
    E = w_router.shape[1]
    wr = jnp.pad(w_router, ((0, 0), (0, epad - E)))
    wrh = wr.astype(BF16)
    wrl = (wr - wrh.astype(F32)).astype(BF16)
    row = lambda i: (i, 0)
    const = lambda *shape: pl.BlockSpec(shape, lambda i: (0,) * len(shape))
    return pl.pallas_call(
        _merge_kernel,
        grid=(T // tm,),
        in_specs=[
            pl.BlockSpec((tm, D), row), pl.BlockSpec((tm, D), row),
            pl.BlockSpec((tm, D), lambda i: (i, gc)), pl.BlockSpec((tm, D), lambda i: (i, gc + 1)),
            pl.BlockSpec((tm, D), row),
            const(D, D), const(D, D), const(D, D), const(1, D), const(D, epad), const(D, epad),
        ],
        out_specs=[pl.BlockSpec((tm, D), row), pl.BlockSpec((tm, D), row), pl.BlockSpec((tm, epad), row)],
        out_shape=[jax.ShapeDtypeStruct((T, D), F32), jax.ShapeDtypeStruct((T, D), BF16),
                   jax.ShapeDtypeStruct((T, epad), F32)],
        compiler_params=_params(("parallel",)),
        name="merge",
    )(ua2, zb2, proj2, proj2, x2, wa, wb, wo, g_ffn.reshape(1, D), wrh, wrl)


TOKEN_BLOCK = 512


def _route_kernel(lg_ref, tri_ref, lmat_ref, affrow_ref, srow_ref, scol_ref, cnt_ref, *, n_exp, cap):
    S, EP = lg_ref.shape
    nchunk = S // 128
    lane = lax.broadcasted_iota(jnp.int32, (S, EP), 1)
    lg = jnp.where(lane < n_exp, lg_ref[...], -1e30)
    e = jnp.exp(lg - jnp.max(lg, axis=-1, keepdims=True))
    aff = e / jnp.sum(e, axis=-1, keepdims=True)
    rows = [aff[c * 128:(c + 1) * 128, :].T[:n_exp, :] for c in range(nchunk)]
    ar = jnp.concatenate(rows, axis=1)
    affrow_ref[...] = ar

    def count_ge(thr):
        return jnp.sum(jnp.where(ar >= thr, 1.0, 0.0), axis=-1, keepdims=True)

    def cond(state):
        lo, hi, it = state
        mid = 0.5 * (lo + hi)
        open_ = jnp.logical_and(mid != lo, mid != hi)
        return jnp.logical_and(jnp.max(jnp.where(open_, 1.0, 0.0)) > 0.0, it < 400)

    def body(state):
        lo, hi, it = state
        mid = 0.5 * (lo + hi)
        ok = count_ge(mid) >= cap
        return jnp.where(ok, mid, lo), jnp.where(ok, hi, mid), it + 1

    lo0 = jnp.zeros((n_exp, 1), F32)
    hi0 = jnp.full((n_exp, 1), 2.0, F32)
    thr, _, _ = lax.while_loop(cond, body, (lo0, hi0, jnp.int32(0)))

    def cumsum_excl(x):
        xs = jnp.concatenate([x[:, c * 128:(c + 1) * 128] for c in range(nchunk)], axis=0)
        cs = jnp.dot(xs.astype(BF16), tri_ref[...], preferred_element_type=F32)
        off = jnp.dot(lmat_ref[...], cs.astype(BF16), preferred_element_type=F32)[:, 127:128]
        inc = cs + off
        return jnp.concatenate([inc[c * n_exp:(c + 1) * n_exp, :] for c in range(nchunk)], axis=1) - x

    gt = ar > thr
    eq = ar == thr
    need = cap - jnp.sum(jnp.where(gt, 1.0, 0.0), axis=-1, keepdims=True)
    tie_rank = cumsum_excl(jnp.where(eq, 1.0, 0.0))
    sel = jnp.logical_or(gt, jnp.logical_and(eq, tie_rank < need))
    self_ = jnp.where(sel, 1.0, 0.0)
    pos = cumsum_excl(self_)
    slot = jnp.where(sel, pos, -1.0)
    srow_ref[...] = slot.astype(jnp.int32)
    clane = lax.broadcasted_iota(jnp.int32, (n_exp, 128), 1)
    cnt = jnp.where(clane == S // TOKEN_BLOCK, float(cap), 0.0)
    for k in range(S // TOKEN_BLOCK):
        cnt = jnp.where(clane == k, pos[:, k * TOKEN_BLOCK:k * TOKEN_BLOCK + 1], cnt)
    cnt_ref[...] = cnt.astype(jnp.int32)
    slot_p = jnp.concatenate([slot, jnp.full((EP - n_exp, S), -1.0, F32)], axis=0)
    cols = [slot_p[:, c * 128:(c + 1) * 128].T for c in range(nchunk)]
    scol_ref[...] = jnp.concatenate(cols, axis=0).astype(jnp.int32)


def _route(logits3, n_exp, cap):
    B, S, EP = logits3.shape
    nchunk = S // 128
    i = np.arange(128)
    tri = (i[:, None] <= i[None, :]).astype(np.float32)
    r = np.arange(nchunk * n_exp)
    lmat = ((r[:, None] % n_exp == r[None, :] % n_exp) & (r[None, :] // n_exp < r[:, None] // n_exp)).astype(np.float32)
    const = lambda *shape: pl.BlockSpec(shape, lambda b: (0,) * len(shape))
    return pl.pallas_call(
        functools.partial(_route_kernel, n_exp=n_exp, cap=cap),
        grid=(B,),
        in_specs=[pl.BlockSpec((None, S, EP), lambda b: (b, 0, 0)), const(128, 128), const(*lmat.shape)],
        out_specs=[
            pl.BlockSpec((None, n_exp, S), lambda b: (b, 0, 0)),
            pl.BlockSpec((None, n_exp, S), lambda b: (b, 0, 0)),
            pl.BlockSpec((None, S, EP), lambda b: (b, 0, 0)),
            pl.BlockSpec((None, n_exp, 128), lambda b: (b, 0, 0)),
        ],
        out_shape=[
            jax.ShapeDtypeStruct((B, n_exp, S), F32),
            jax.ShapeDtypeStruct((B, n_exp, S), jnp.int32), jax.ShapeDtypeStruct((B, S, EP), jnp.int32),
            jax.ShapeDtypeStruct((B, n_exp, 128), jnp.int32),
        ],
        compiler_params=_params(("parallel",)),
        name="route",
    )(logits3, jnp.asarray(tri, BF16), jnp.asarray(lmat, BF16))


GATHER_ROWS = 128


def _gather_kernel(cnt_ref, h_ref, srow_ref, affrow_ref, wg_ref, wu_ref, wd_ref,
                   xe_ref, val_ref, wgo_ref, wuo_ref, wdo_ref, acc_ref, vacc_ref, *, cap):
    b = pl.program_id(0)
    e = pl.program_id(1)
    S = h_ref.shape[0]
    nblk = S // TOKEN_BLOCK
    base = (b * pl.num_programs(1) + e) * (nblk + 1)
    R = GATHER_ROWS

    wgo_ref[...] = wg_ref[...].astype(BF16)
    wuo_ref[...] = wu_ref[...].astype(BF16)
    wdo_ref[...] = wd_ref[...].astype(BF16)

    acc_ref[...] = jnp.zeros_like(acc_ref)
    vacc_ref[...] = jnp.zeros_like(vacc_ref)
    row = lax.broadcasted_iota(jnp.int32, (R, TOKEN_BLOCK), 0)

    def window(k, w0, first):
        tok = slice(k * TOKEN_BLOCK, (k + 1) * TOKEN_BLOCK)
        idx = row + w0
        hit = idx == srow_ref[pl.ds(e, 1), tok]
        if first is not None:
            hit = jnp.logical_and(hit, idx >= first)
        rows = pl.ds(pl.multiple_of(w0, F32_ROWS), R)
        acc_ref[rows, :] += jnp.dot(jnp.where(hit, 1.0, 0.0).astype(BF16), h_ref[tok, :],
                                    preferred_element_type=F32)
        vacc_ref[rows, :] += jnp.sum(jnp.where(hit, affrow_ref[pl.ds(e, 1), tok], 0.0), axis=-1, keepdims=True)

    w0 = [jnp.minimum((cnt_ref[base + k] // F32_ROWS) * F32_ROWS, cap - R) for k in range(nblk)]
    for k in range(nblk):
        window(k, w0[k], None)
    for k in range(nblk):
        def more(w, k=k):
            window(k, jnp.minimum(w, cap - R), w)
            return w + R

        lax.while_loop(lambda w, hi=cnt_ref[base + k + 1]: w < hi, more, w0[k] + R)

    xe_ref[...] = acc_ref[...].astype(xe_ref.dtype)
    val_ref[...] = vacc_ref[...]


def _gather(h3, srow, affrow, cnt, cap, wg, wu, wd):
    B, S, D = h3.shape
    E = srow.shape[1]
    F = wg.shape[2]
    nstep = B * E
    wg2, wu2, wd2 = wg.reshape(E * D, F), wu.reshape(E * D, F), wd.reshape(E * F, D)
    step = lambda b, e, cnt: (b * E + e, 0)
    outs = pl.pallas_call(
        functools.partial(_gather_kernel, cap=cap),
        grid_spec=pltpu.PrefetchScalarGridSpec(
            num_scalar_prefetch=1,
            grid=(B, E),
            in_specs=[
                pl.BlockSpec((None, S, D), lambda b, e, cnt: (b, 0, 0)),
                pl.BlockSpec((None, E, S), lambda b, e, cnt: (b, 0, 0)),
                pl.BlockSpec((None, E, S), lambda b, e, cnt: (b, 0, 0)),
                pl.BlockSpec((E * D // nstep, F), step),
                pl.BlockSpec((E * D // nstep, F), step),
                pl.BlockSpec((E * F // nstep, D), step),
            ],
            out_specs=[
                pl.BlockSpec((None, None, cap, D), lambda b, e, cnt: (b, e, 0, 0)),
                pl.BlockSpec((None, None, cap, 1), lambda b, e, cnt: (b, e, 0, 0)),
                pl.BlockSpec((E * D // nstep, F), step),
                pl.BlockSpec((E * D // nstep, F), step),
                pl.BlockSpec((E * F // nstep, D), step),
            ],
            scratch_shapes=[pltpu.VMEM((cap, D), F32), pltpu.VMEM((cap, 1), F32)],
        ),
        out_shape=[
            jax.ShapeDtypeStruct((B, E, cap, D), BF16), jax.ShapeDtypeStruct((B, E, cap, 1), F32),
            jax.ShapeDtypeStruct((E * D, F), BF16), jax.ShapeDtypeStruct((E * D, F), BF16),
            jax.ShapeDtypeStruct((E * F, D), BF16),
        ],
        compiler_params=_params(("arbitrary", "arbitrary")),
        name="gather",
    )(cnt, h3, srow, affrow, wg2, wu2, wd2)
    xe, vals, wgb, wub, wdb = outs
    return xe, vals, wgb.reshape(E, D, F), wub.reshape(E, D, F), wdb.reshape(E, F, D)


def _ffn_kernel(xe_ref, val_ref, wg_ref, wu_ref, wd_ref, ye_ref):
    xe = xe_ref[...]
    gt = jnp.dot(xe, wg_ref[...], preferred_element_type=F32)
    up = jnp.dot(xe, wu_ref[...], preferred_element_type=F32)
    act = (gt * _sigmoid(gt) * up).astype(BF16)
    ye = jnp.dot(act, wd_ref[...], preferred_element_type=F32)
    ye_ref[...] = (ye * val_ref[...]).astype(ye_ref.dtype)


def _ffn(xe, vals, wg, wu, wd):
    B, E, C, D = xe.shape
    F = wg.shape[2]
    return pl.pallas_call(
        _ffn_kernel,
        grid=(E, B),
        in_specs=[
            pl.BlockSpec((None, None, C, D), lambda e, b: (b, e, 0, 0)),
            pl.BlockSpec((None, None, C, 1), lambda e, b: (b, e, 0, 0)),
            pl.BlockSpec((None, D, F), lambda e, b: (e, 0, 0)),
            pl.BlockSpec((None, D, F), lambda e, b: (e, 0, 0)),
            pl.BlockSpec((None, F, D), lambda e, b: (e, 0, 0)),
        ],
        out_specs=pl.BlockSpec((None, None, C, D), lambda e, b: (b, e, 0, 0)),
        out_shape=jax.ShapeDtypeStruct((B, E, C, D), BF16),
        compiler_params=_params(("parallel", "arbitrary")),
        name="ffn",
    )(xe, vals, wg, wu, wd)


COMBINE_SLOTS = 256


def _combine_kernel(cnt_ref, ye_ref, scol_ref, x1_ref, g_ref, o_ref, extra_ref, *, n_exp, cap):
    b = pl.program_id(0)
    r = pl.program_id(1)
    nblk1 = pl.num_programs(1) + 1
    tm = x1_ref.shape[0]
    KW = COMBINE_SLOTS
    lane = lax.broadcasted_iota(jnp.int32, (tm, KW), 1)
    scol = scol_ref[...]
    extra_ref[...] = jnp.zeros_like(extra_ref)

    def window(e, w0, first):
        idx = lane + w0
        hit = idx == scol[:, e:e + 1]
        if first is not None:
            hit = jnp.logical_and(hit, idx >= first)
        ye = ye_ref[e, pl.ds(pl.multiple_of(w0, BF16_ROWS), KW), :]
        return jnp.dot(jnp.where(hit, 1.0, 0.0).astype(BF16), ye, preferred_element_type=F32)

    base = [(b * n_exp + e) * nblk1 + r for e in range(n_exp)]
    w0 = [jnp.minimum((cnt_ref[base[e]] // BF16_ROWS) * BF16_ROWS, cap - KW) for e in range(n_exp)]
    acc = x1_ref[...]
    for e in range(n_exp):
        acc = acc + window(e, w0[e], None)
    for e in range(n_exp):
        def more(w, e=e):
            extra_ref[...] += window(e, jnp.minimum(w, cap - KW), w)
            return w + KW

        lax.while_loop(lambda w, hi=cnt_ref[base[e] + 1]: w < hi, more, w0[e] + KW)

    acc = acc + extra_ref[...]
    y = acc * lax.rsqrt(jnp.mean(acc * acc, axis=-1, keepdims=True) + EPS)
    o_ref[...] = y * g_ref[...]


def _combine(ye, scol, cnt, x13, g_final):
    B, E, C, D = ye.shape
    S = x13.shape[1]
    EP = scol.shape[2]
    tm = TOKEN_BLOCK
    return pl.pallas_call(
        functools.partial(_combine_kernel, n_exp=E, cap=C),
        grid_spec=pltpu.PrefetchScalarGridSpec(
            num_scalar_prefetch=1,
            grid=(B, S // tm),
            in_specs=[
                pl.BlockSpec((None, E, C, D), lambda b, r, cnt: (b, 0, 0, 0)),
                pl.BlockSpec((None, tm, EP), lambda b, r, cnt: (b, r, 0)),
                pl.BlockSpec((None, tm, D), lambda b, r, cnt: (b, r, 0)),
                pl.BlockSpec((1, D), lambda b, r, cnt: (0, 0)),
            ],
            out_specs=pl.BlockSpec((None, tm, D), lambda b, r, cnt: (b, r, 0)),
            scratch_shapes=[pltpu.VMEM((tm, D), F32)],
        ),
        out_shape=jax.ShapeDtypeStruct((B, S, D), F32),
        compiler_params=_params(("arbitrary", "arbitrary")),
        name="combine",
    )(cnt, ye, scol, x13, g_final.reshape(1, D))


def kernel(x, g_mix, w_in, conv_a_w, conv_a_b, lru_w_r, lru_b_r, lru_w_i, lru_b_i, lru_lambda, w_a_out, conv_b_w, conv_b_b, filt_w1, filt_b1, filt_w2, filt_b2, filt_w3, filt_freq, filt_bias, w_b_out, w_o, g_ffn, w_router, w_gate, w_up, w_down, g_final):
    B, S, D = x.shape
    assert w_in.shape[0] == 1, "single-layer block only"
    l = 0
    lru_w = conv_a_w.shape[2]
    hy_w = filt_bias.shape[2]
    n_exp = w_router.shape[2]
    cap = CAPACITY_FACTOR * S // n_exp
    assert 2 * S == N_FFT
    ma32, ma64, gb, gbi, mai = (jnp.asarray(c).astype(BF16) for c in _dft_constants())

    x2 = x.reshape(B * S, D)
    proj2 = _inproj(x2, g_mix[l], w_in[l])
    proj3 = proj2.reshape(B, S, -1)
    ua = _rglru(proj3, conv_a_w[l], conv_a_b[l], lru_w_r[l], lru_b_r[l], lru_w_i[l], lru_b_i[l], lru_lambda[l])
    kf = _filter_spectra(S, filt_w1[l], filt_b1[l], filt_w2[l], filt_b2[l], filt_w3[l], filt_freq[l], hy_w, ma64, gb)
    zb = _hyena(proj3, 2 * lru_w, conv_b_w[l], conv_b_b[l], filt_bias[l], kf, ma32, gb, gbi, mai)
    x1, h, logits = _merge(ua.reshape(B * S, lru_w), zb.reshape(B * S, hy_w), proj2, 2 * lru_w + 3 * hy_w, x2,
                           w_a_out[l].astype(BF16), w_b_out[l].astype(BF16), w_o[l].astype(BF16), g_ffn[l],
                           w_router[l])
    affrow, srow, scol, cnt = _route(logits.reshape(B, S, -1), n_exp, cap)
    cnt = cnt[:, :, :S // TOKEN_BLOCK + 1].reshape(-1)
    xe, vals, wg, wu, wd = _gather(h.reshape(B, S, D), srow, affrow, cnt, cap, w_gate[l], w_up[l], w_down[l])
    ye = _ffn(xe, vals, wg, wu, wd)
    return _combine(ye, scol, cnt, x1.reshape(B, S, D), g_final)
```

```python
import functools
import math

import numpy as np
import jax
import jax.numpy as jnp
from jax import lax
from jax.experimental import pallas as pl
from jax.experimental.pallas import tpu as pltpu

F32 = jnp.float32
BF16 = jnp.bfloat16
EPS = 1e-6
LOG2E = 1.4426950408889634

VMEM_LIMIT_BYTES = 56 * 1024 * 1024
BF16_ROWS = 16
F32_ROWS = 8
LANES = 128

LRU_HEADS = 16
LRU_C = 8.0
CAPACITY_FACTOR = 2
FILTER_BANDS = 16
DECAY_FAST = 0.3
DECAY_SLOW = 1.5
DECAY_TARGET = 1e-2

FFT_N1 = 64
FFT_N2 = 128
N_FFT = FFT_N1 * FFT_N2
FFT_KH = FFT_N1 // 2 + 1
FFT_G = F32_ROWS


def _params(sem):
    return pltpu.CompilerParams(dimension_semantics=sem, vmem_limit_bytes=VMEM_LIMIT_BYTES)


def _inproj_kernel(x_ref, g_ref, w_ref, o_ref, h_ref, wb_ref):
    j = pl.program_id(1)

    @pl.when(pl.program_id(0) == 0)
    def _():
        wb_ref[j] = w_ref[...].astype(BF16)

    @pl.when(j == 0)
    def _():
        x = x_ref[...]
        y = x * lax.rsqrt(jnp.mean(x * x, axis=-1, keepdims=True) + EPS)
        h_ref[...] = (y * g_ref[...]).astype(BF16)

    o_ref[...] = jnp.dot(h_ref[...], wb_ref[j], preferred_element_type=F32).astype(o_ref.dtype)


def _inproj(x2, g, w, tm=1024, tn=1024):
    T, D = x2.shape
    N = w.shape[1]
    nj = N // tn
    return pl.pallas_call(
        _inproj_kernel,
        grid=(T // tm, nj),
        in_specs=[
            pl.BlockSpec((tm, D), lambda i, j: (i, 0)),
            pl.BlockSpec((1, D), lambda i, j: (0, 0)),
            pl.BlockSpec((D, tn), lambda i, j: (0, jnp.where(i == 0, j, nj - 1))),
        ],
        out_specs=pl.BlockSpec((tm, tn), lambda i, j: (i, j)),
        out_shape=jax.ShapeDtypeStruct((T, N), BF16),
        scratch_shapes=[pltpu.VMEM((tm, D), BF16), pltpu.VMEM((nj, D, tn), BF16)],
        compiler_params=_params(("arbitrary", "arbitrary")),
        name="inproj",
    )(x2, g.reshape(1, D), w)


PAD = BF16_ROWS


def _stage_padded(src_ref, pad_ref):
    S, W = src_ref.shape
    pad_ref[0:PAD, :] = jnp.zeros((PAD, W), pad_ref.dtype)
    pad_ref[PAD + S:PAD + S + PAD, :] = jnp.zeros((PAD, W), pad_ref.dtype)
    pad_ref[PAD:PAD + S, :] = src_ref[...]


def _conv_chunk(pad_ref, r0, rows, taps, w_ref, b_ref):
    n = rows + 2 * PAD
    win = pad_ref[pl.ds(r0, n), :].astype(F32)
    acc = None
    for k, off in enumerate(taps):
        sh = win if off == 0 else pltpu.roll(win, (-off) % n, 0)
        term = sh[PAD:PAD + rows, :] * w_ref[k:k + 1, :]
        acc = term if acc is None else acc + term
    return acc + b_ref[...]


NSEG = F32_ROWS
SCAN_UNROLL = 8


def _sigmoid(x):
    return 1.0 / (1.0 + jnp.exp2(x * (-LOG2E)))


def _sigmoid_tanh(x):
    return 0.5 * jnp.tanh(0.5 * x) + 0.5


def _gelu_tanh(x):
    return 0.5 * x * (1.0 + jnp.tanh(math.sqrt(2.0 / math.pi) * (x + 0.044715 * (x * x * x))))


def _softplus(x):
    return jnp.maximum(x, 0.0) + jnp.log(1.0 + jnp.exp(-jnp.abs(x)))


def _ld(ref, rows):
    return jnp.concatenate([ref[l, rows, :] for l in range(ref.shape[0])], axis=1)


def _st(ref, rows, v):
    for l in range(ref.shape[0]):
        ref[l, rows, :] = v[:, l * LANES:(l + 1) * LANES]


def _rglru_kernel(ax_ref, gate_ref, cw_ref, cb_ref, wcat_ref, bias_ref, lam_ref, wc_ref, o_ref, wco_ref,
                  pad_ref, af_ref, bf_ref, ab_ref, bb_ref, hs_ref):
    wco_ref[...] = wc_ref[...].astype(BF16)
    S, W = ax_ref.shape
    seg = S // NSEG
    pitch = hs_ref.shape[1] // NSEG
    _stage_padded(ax_ref, pad_ref)
    coef = [(LRU_C * LOG2E) * _softplus(-lam_ref[d:d + 1, :]) for d in range(2)]

    def gates(c, _):
        r0 = pl.multiple_of(c * seg, seg)
        xa = _conv_chunk(pad_ref, r0, seg, (-2, -1, 0, 1), cw_ref, cb_ref)
        pre = jnp.dot(xa.astype(BF16), wcat_ref[...], preferred_element_type=F32)
        pos = r0 + lax.broadcasted_iota(jnp.int32, (seg, W), 0)
        for d, (a_ref, b_ref, start) in enumerate(((af_ref, bf_ref, 0), (ab_ref, bb_ref, S - 1))):
            r = _sigmoid_tanh(pre[:, (2 * d) * W:(2 * d + 1) * W] + bias_ref[2 * d:2 * d + 1, :])
            i = _sigmoid_tanh(pre[:, (2 * d + 1) * W:(2 * d + 2) * W] + bias_ref[2 * d + 1:2 * d + 2, :])
            a = jnp.exp2(-coef[d] * r)
            m2 = 1.0 - a * a
            mult = jnp.where(m2 > 0.0, m2 * lax.rsqrt(m2), 0.0)
            mult = jnp.where(pos == start, 1.0, mult)
            _st(a_ref, pl.ds(c, seg, stride=NSEG), a)
            _st(b_ref, pl.ds(c, seg, stride=NSEG), mult * (i * xa))
        return 0

    lax.fori_loop(0, NSEG, gates, 0)

    def scan(u, carry):
        hf, pf, hb, pb = carry
        for k in range(SCAN_UNROLL):
            j = u * SCAN_UNROLL + k
            rf = pl.multiple_of(j * NSEG, NSEG)
            rb = pl.multiple_of((seg - 1 - j) * NSEG, NSEG)
            a = _ld(af_ref, pl.ds(rf, NSEG))
            hf = a * hf + _ld(bf_ref, pl.ds(rf, NSEG))
            pf = a * pf
            _st(bf_ref, pl.ds(rf, NSEG), hf)
            _st(af_ref, pl.ds(rf, NSEG), pf)
            a = _ld(ab_ref, pl.ds(rb, NSEG))
            hb = a * hb + _ld(bb_ref, pl.ds(rb, NSEG))
            pb = a * pb
            _st(bb_ref, pl.ds(rb, NSEG), hb)
            _st(ab_ref, pl.ds(rb, NSEG), pb)
        return hf, pf, hb, pb

    zero = jnp.zeros((NSEG, W), F32)
    one = jnp.ones((NSEG, W), F32)
    hf, pf, hb, pb = lax.fori_loop(0, seg // SCAN_UNROLL, scan, (zero, one, zero, one))

    row = lax.broadcasted_iota(jnp.int32, (NSEG, W), 0)
    cf = zero
    cb = zero
    for _ in range(NSEG - 1):
        cf = jnp.where(row >= 1, pltpu.roll(hf + pf * cf, 1, 0), 0.0)
        cb = jnp.where(row < NSEG - 1, pltpu.roll(hb + pb * cb, NSEG - 1, 0), 0.0)

    def fix(u, _):
        for k in range(SCAN_UNROLL):
            j = u * SCAN_UNROLL + k
            r = pl.multiple_of(j * NSEG, NSEG)
            rows = pl.ds(r, NSEG)
            h = _ld(bf_ref, rows) + _ld(af_ref, rows) * cf + _ld(bb_ref, rows) + _ld(ab_ref, rows) * cb
            _st(hs_ref, pl.ds(j, NSEG, stride=pitch), h)
        return 0

    lax.fori_loop(0, seg // SCAN_UNROLL, fix, 0)

    def out(c, _):
        r0 = pl.multiple_of(c * seg, seg)
        h = _ld(hs_ref, pl.ds(pl.multiple_of(c * pitch, F32_ROWS), seg))
        g = gate_ref[pl.ds(r0, seg), :].astype(F32)
        o_ref[pl.ds(r0, seg), :] = (h * _gelu_tanh(g)).astype(o_ref.dtype)
        return 0

    lax.fori_loop(0, NSEG, out, 0)


def _rglru(proj3, conv_w, conv_b, w_r, b_r, w_i, b_i, lam, wcast, wt=256):
    B, S, _ = proj3.shape
    W = conv_w.shape[1]
    nj = W // wt
    crow = wcast.shape[0] // (B * nj)
    cstep = lambda b, j: (b * nj + j, 0)
    hd = W // LRU_HEADS
    hpt = wt // hd
    pitch = S // NSEG + F32_ROWS

    def blockdiag(w):
        w4 = w.reshape(nj, hpt, hd, hd)
        eye = jnp.eye(hpt, dtype=w.dtype)
        return jnp.einsum('jhab,hg->jhagb', w4, eye).reshape(nj, wt, wt)

    wcat = jnp.concatenate([blockdiag(w_r[0]), blockdiag(w_i[0]), blockdiag(w_r[1]), blockdiag(w_i[1])],
                           axis=-1).astype(BF16)
    bias = jnp.stack([b_r[0], b_i[0], b_r[1], b_i[1]], axis=0)
    gate_off = W // wt
    return pl.pallas_call(
        _rglru_kernel,
        grid=(B, nj),
        in_specs=[
            pl.BlockSpec((None, S, wt), lambda b, j: (b, 0, j)),
            pl.BlockSpec((None, S, wt), lambda b, j: (b, 0, gate_off + j)),
            pl.BlockSpec((conv_w.shape[0], wt), lambda b, j: (0, j)),
            pl.BlockSpec((1, wt), lambda b, j: (0, j)),
            pl.BlockSpec((None, wt, 4 * wt), lambda b, j: (j, 0, 0)),
            pl.BlockSpec((4, wt), lambda b, j: (0, j)),
            pl.BlockSpec((2, wt), lambda b, j: (0, j)),
            pl.BlockSpec((crow, wcast.shape[1]), cstep),
        ],
        out_specs=[pl.BlockSpec((None, S, wt), lambda b, j: (b, 0, j)), pl.BlockSpec((crow, wcast.shape[1]), cstep)],
        out_shape=[jax.ShapeDtypeStruct((B, S, W), BF16), jax.ShapeDtypeStruct(wcast.shape, BF16)],
        scratch_shapes=[pltpu.VMEM((S + 2 * PAD, wt), BF16)] + [pltpu.VMEM((wt // LANES, S, LANES), F32)] * 4
        + [pltpu.VMEM((wt // LANES, NSEG * pitch, LANES), F32)],
        compiler_params=_params(("parallel", "parallel")),
        name="rglru",
    )(proj3, proj3, conv_w, conv_b.reshape(1, W), wcat, bias, lam, wcast)


COMPS_FWD = [(c, k1) for c in range(2) for k1 in range(FFT_KH)]
COMPS_INV = [(c, k1) for (c, k1) in COMPS_FWD if not (c == 1 and k1 in (0, FFT_KH - 1))]


@functools.lru_cache(maxsize=None)
def _dft_constants():
    g = FFT_G
    eye = np.eye(g)

    def stage_a(n1_in):
        k1 = np.arange(FFT_KH)[:, None]
        n1 = np.arange(n1_in)[None, :]
        ang = -2.0 * np.pi * ((k1 * n1) % FFT_N1) / FFT_N1
        a = np.stack([np.cos(ang), np.sin(ang)])
        a = np.stack([a[c, k] for (c, k) in COMPS_FWD])
        return np.einsum('qn,st->qsnt', a, eye).reshape(len(COMPS_FWD) * g, n1_in * g)

    k1 = np.arange(FFT_KH)[:, None, None]
    k2 = np.arange(FFT_N2)[None, :, None]
    n2 = np.arange(FFT_N2)[None, None, :]
    ang = -2.0 * np.pi * ((n2 * (k1 + FFT_N1 * k2)) % N_FFT) / N_FFT
    gr, gi = np.cos(ang), np.sin(ang)
    gb = np.concatenate([np.concatenate([gr, -gi], axis=2), np.concatenate([gi, gr], axis=2)], axis=1)
    gbi = np.transpose(gb, (0, 2, 1))

    c = np.full(FFT_KH, 2.0)
    c[0] = 1.0
    c[-1] = 1.0
    n1 = np.arange(FFT_N1 // 2)[:, None]
    kk = np.arange(FFT_KH)[None, :]
    ang = 2.0 * np.pi * ((n1 * kk) % FFT_N1) / FFT_N1
    bi = np.stack([np.cos(ang) * c / N_FFT, -np.sin(ang) * c / N_FFT])
    bi = np.stack([bi[c_, :, k] for (c_, k) in COMPS_INV], axis=1)
    mai = np.einsum('nq,st->nsqt', bi, eye).reshape((FFT_N1 // 2) * g, len(COMPS_INV) * g)
    f = np.float32
    return stage_a(FFT_N1 // 2).astype(f), stage_a(FFT_N1).astype(f), gb.astype(f), gbi.astype(f), mai.astype(f)


def _split_rows(tiles):
    f = [t.astype(F32) for t in tiles]
    return [jnp.concatenate([t[h * FFT_G:(h + 1) * FFT_G] for t in f], axis=0).astype(BF16)
            for h in range(BF16_ROWS // FFT_G)]


def _join_rows(outs, q):
    return jnp.concatenate([o[q * FFT_G:(q + 1) * FFT_G] for o in outs], axis=0)


def _stage_a_fwd(src_ref, ma_ref, y_ref, n1_in):
    for p in range(FFT_N2 // BF16_ROWS):
        r = BF16_ROWS * p
        opnds = _split_rows([src_ref[FFT_N2 * n1 + r:FFT_N2 * n1 + r + BF16_ROWS, :] for n1 in range(n1_in)])
        outs = [jnp.dot(ma_ref[...], o, preferred_element_type=F32) for o in opnds]
        for q, (c, k1) in enumerate(COMPS_FWD):
            y_ref[k1, c * FFT_N2 + r:c * FFT_N2 + r + BF16_ROWS, :] = _join_rows(outs, q).astype(y_ref.dtype)


def _stage_a_inv(y_ref, mai_ref, p):
    r = BF16_ROWS * p
    opnds = _split_rows([y_ref[k1, c * FFT_N2 + r:c * FFT_N2 + r + BF16_ROWS, :] for (c, k1) in COMPS_INV])
    outs = [jnp.dot(mai_ref[...], o, preferred_element_type=F32) for o in opnds]
    return [_join_rows(outs, n1) for n1 in range(FFT_N1 // 2)]


def _stage_b_fwd_slab(gb_ref, y_ref, k1):
    return jnp.dot(gb_ref[k1], y_ref[k1], preferred_element_type=F32)


def _dot3(a, b):
    ah = a.astype(BF16)
    al = (a - ah.astype(F32)).astype(BF16)
    bh = b.astype(BF16)
    bl = (b - bh.astype(F32)).astype(BF16)
    d = lambda u, v: jnp.dot(u, v, preferred_element_type=F32)
    return d(ah, bh) + d(al, bh) + d(ah, bl)


def _filtmlp_kernel(fa_ref, fb_ref, w1_ref, b1_ref, w2_ref, b2_ref, freq_ref, o_ref):
    u = jnp.concatenate([_dot3(fa_ref[...], w1_ref[...]), _dot3(fb_ref[...], w1_ref[...])], axis=1)
    h = jnp.sin(freq_ref[0:1, :] * (u + b1_ref[...]))
    o_ref[...] = jnp.sin(freq_ref[1:2, :] * (_dot3(h, w2_ref[...]) + b2_ref[...]))


def _filtfft_kernel(h_ref, t_ref, w3f_ref, w3b_ref, delta_ref, ma_ref, gb_ref, o_ref, kc_ref, y_ref, *, chunk):
    n, W = kc_ref.shape
    L = n // 2

    def body(c, _):
        r0 = pl.multiple_of(c * chunk, chunk)
        h = h_ref[pl.ds(r0, chunk), :].astype(BF16)
        kf = jnp.dot(h, w3f_ref[...], preferred_element_type=F32)
        kb = jnp.dot(h, w3b_ref[...], preferred_element_type=F32)
        row = r0 + lax.broadcasted_iota(jnp.int32, (chunk, W), 0)
        kb = jnp.where(row == 0, 0.0, kb)
        kc_ref[pl.ds(r0, chunk), :] = (kf * jnp.exp(-t_ref[pl.ds(r0, chunk), :] * delta_ref[...])).astype(kc_ref.dtype)
        kc_ref[pl.ds(L + r0, chunk), :] = (kb * jnp.exp(-t_ref[pl.ds(L + r0, chunk), :] * delta_ref[...])).astype(kc_ref.dtype)
        return 0

    lax.fori_loop(0, L // chunk, body, 0)
    _stage_a_fwd(kc_ref, ma_ref, y_ref, FFT_N1)

    def slab(k1, _):
        o_ref[k1] = _stage_b_fwd_slab(gb_ref, y_ref, k1).astype(o_ref.dtype)
        return 0

    lax.fori_loop(0, FFT_KH, slab, 0, unroll=11)


def _filter_spectra(L, w1, b1, w2, b2, w3, freq, W, ma64, gb, wt=256, chunk=512):
    f32 = F32
    order = w3.shape[1] // (2 * W)
    hid = w1.shape[1]
    t = jnp.linspace(0.0, 1.0, L, dtype=f32)[:, None]
    w = (2.0 * math.pi / L) * jnp.arange(L, dtype=f32)[:, None]
    f = jnp.linspace(1e-4, FILTER_BANDS - 1, FILTER_BANDS, dtype=f32)[None, :]
    z = jnp.concatenate([t, jnp.cos(w * f), -jnp.sin(w * f)], axis=-1)
    emb = z.shape[1]
    embp = 128
    z = jnp.pad(z, ((0, 0), (0, embp - emb)))
    feat = jnp.concatenate([z, z[:1], z[:0:-1]], axis=0)
    tt = jnp.concatenate([t, t[:1], t[:0:-1]], axis=0)
    w1p = jnp.pad(w1, ((0, embp - emb), (0, 0)))
    zero = jnp.zeros_like(w2)
    w2d = jnp.concatenate([jnp.concatenate([w2, zero], axis=1), jnp.concatenate([zero, w2], axis=1)], axis=0)
    two = lambda v: jnp.concatenate([v, v], axis=-1)
    nblk = L // chunk
    const1 = lambda *shape: pl.BlockSpec(shape, lambda i: (0,) * len(shape))
    hfeat = pl.pallas_call(
        _filtmlp_kernel,
        grid=(nblk,),
        in_specs=[
            pl.BlockSpec((chunk, embp), lambda i: (i, 0)), pl.BlockSpec((chunk, embp), lambda i: (nblk + i, 0)),
            const1(embp, hid), const1(1, 2 * hid), const1(2 * hid, 2 * hid), const1(1, 2 * hid), const1(2, 2 * hid),
        ],
        out_specs=pl.BlockSpec((chunk, 2 * hid), lambda i: (i, 0)),
        out_shape=jax.ShapeDtypeStruct((L, 2 * hid), f32),
        compiler_params=_params(("parallel",)),
        name="filtmlp",
    )(feat, feat, w1p, two(b1.reshape(1, hid)), w2d, two(b2.reshape(1, hid)), two(freq))

    w3b16 = w3.astype(BF16)
    zero3 = jnp.zeros_like(w3b16)
    w3f = jnp.concatenate([w3b16, zero3], axis=0)
    w3b = jnp.concatenate([zero3, w3b16], axis=0)
    min_decay = math.log(DECAY_TARGET) / DECAY_SLOW
    max_decay = math.log(DECAY_TARGET) / DECAY_FAST
    deltas = jnp.abs(jnp.linspace(min_decay, max_decay, W, dtype=f32))[None, :]
    nj = W // wt
    n = 2 * L
    const = lambda *shape: pl.BlockSpec(shape, lambda o, j: (0,) * len(shape))
    return pl.pallas_call(
        functools.partial(_filtfft_kernel, chunk=chunk),
        grid=(order, nj),
        in_specs=[
            const(L, 2 * hid), const(n, 1),
            pl.BlockSpec((2 * hid, wt), lambda o, j: (0, (2 * o) * nj + j)),
            pl.BlockSpec((2 * hid, wt), lambda o, j: (0, (2 * o + 1) * nj + j)),
            pl.BlockSpec((1, wt), lambda o, j: (0, j)),
            const(*ma64.shape), const(*gb.shape),
        ],
        out_specs=pl.BlockSpec((None, None, FFT_KH, 2 * FFT_N2, wt), lambda o, j: (o, j, 0, 0, 0)),
        out_shape=jax.ShapeDtypeStruct((order, nj, FFT_KH, 2 * FFT_N2, wt), BF16),
        scratch_shapes=[pltpu.VMEM((n, wt), BF16), pltpu.VMEM((FFT_KH, 2 * FFT_N2, wt), BF16)],
        compiler_params=_params(("parallel", "parallel")),
        name="filtfft",
    )(hfeat, tt, w3f, w3b, deltas, ma64, gb)


def _hyena_kernel(v_ref, x1_ref, x2_ref, cw_ref, cb_ref, fb_ref, kf_ref, ma_ref, gb_ref, gbi_ref, mai_ref, o_ref,
                  pad_ref, z_ref, g1_ref, g2_ref, y_ref, *, chunk):
    S, W = v_ref.shape
    g = BF16_ROWS

    for idx, (src, dst) in enumerate(((v_ref, z_ref), (x1_ref, g1_ref), (x2_ref, g2_ref))):
        _stage_padded(src, pad_ref)

        def conv(c, _, idx=idx, dst=dst):
            r0 = pl.multiple_of(c * chunk, chunk)
            y = _conv_chunk(pad_ref, r0, chunk, (-1, 0, 1), cw_ref.at[idx], cb_ref.at[idx])
            dst[pl.ds(r0, chunk), :] = y.astype(dst.dtype)
            return 0

        lax.fori_loop(0, S // chunk, conv, 0)

    for order, (gate_ref, dst) in enumerate(((g1_ref, z_ref), (g2_ref, o_ref))):
        _stage_a_fwd(z_ref, ma_ref, y_ref, FFT_N1 // 2)

        def slab(k1, _, order=order):
            x = _stage_b_fwd_slab(gb_ref, y_ref, k1)
            kf = kf_ref[order, k1].astype(F32)
            xr, xi = x[:FFT_N2], x[FFT_N2:]
            kr, ki = kf[:FFT_N2], kf[FFT_N2:]
            p = jnp.concatenate([xr * kr - xi * ki, xr * ki + xi * kr], axis=0).astype(BF16)
            y_ref[k1] = jnp.dot(gbi_ref[k1], p, preferred_element_type=F32).astype(y_ref.dtype)
            return 0

        lax.fori_loop(0, FFT_KH, slab, 0, unroll=True)

        bias = fb_ref[order:order + 1, :]
        for p in range(FFT_N2 // g):
            for n1, y in enumerate(_stage_a_inv(y_ref, mai_ref, p)):
                t0 = FFT_N2 * n1 + g * p
                zt = z_ref[t0:t0 + g, :].astype(F32)
                gt = gate_ref[t0:t0 + g, :].astype(F32)
                dst[t0:t0 + g, :] = (gt * (y + bias * zt)).astype(dst.dtype)


def _hyena(proj3, col0, conv_w, conv_b, filt_bias, kf, ma32, gb, gbi, mai, wt=256, chunk=512):
    B, S, _ = proj3.shape
    W = filt_bias.shape[1]
    nj = W // wt
    c0 = col0 // wt
    cw = conv_w.reshape(conv_w.shape[0], 3, W).transpose(1, 0, 2)
    cb = conv_b.reshape(3, 1, W)
    const = lambda *shape: pl.BlockSpec(shape, lambda j, b: (0,) * len(shape), pipeline_mode=pl.Buffered(1))
    return pl.pallas_call(
        functools.partial(_hyena_kernel, chunk=chunk),
        grid=(nj, B),
        in_specs=[
            pl.BlockSpec((None, S, wt), lambda j, b: (b, 0, c0 + j)),
            pl.BlockSpec((None, S, wt), lambda j, b: (b, 0, c0 + nj + j)),
            pl.BlockSpec((None, S, wt), lambda j, b: (b, 0, c0 + 2 * nj + j)),
            pl.BlockSpec((3, conv_w.shape[0], wt), lambda j, b: (0, 0, j)),
            pl.BlockSpec((3, 1, wt), lambda j, b: (0, 0, j)),
            pl.BlockSpec((filt_bias.shape[0], wt), lambda j, b: (0, j)),
            pl.BlockSpec((kf.shape[0], None, FFT_KH, 2 * FFT_N2, wt), lambda j, b: (0, j, 0, 0, 0),
                         pipeline_mode=pl.Buffered(1)),
            const(*ma32.shape), const(*gb.shape), const(*gbi.shape), const(*mai.shape),
        ],
        out_specs=pl.BlockSpec((None, S, wt), lambda j, b: (b, 0, j)),
        out_shape=jax.ShapeDtypeStruct((B, S, W), BF16),
        scratch_shapes=[pltpu.VMEM((S + 2 * PAD, wt), BF16)] + [pltpu.VMEM((S, wt), BF16)] * 3
        + [pltpu.VMEM((FFT_KH, 2 * FFT_N2, wt), BF16)],
        compiler_params=_params(("parallel", "arbitrary")),
        name="hyena",
    )(proj3, proj3, proj3, cw, cb, filt_bias, kf, ma32, gb, gbi, mai)


MERGE_SUB = 2


def _merge_kernel(ua_ref, zb_ref, ga_ref, gbr_ref, x_ref, wa_ref, wb_ref, wo_ref, g_ref, wrh_ref, wrl_ref, wc_ref,
                  x1_ref, h_ref, lg_ref, wco_ref):
    wco_ref[...] = wc_ref[...].astype(BF16)
    sub = x_ref.shape[0] // MERGE_SUB
    for s in range(MERGE_SUB):
        rows = slice(s * sub, (s + 1) * sub)
        ya = jnp.dot(ua_ref[rows, :], wa_ref[...], preferred_element_type=F32)
        yb = jnp.dot(zb_ref[rows, :], wb_ref[...], preferred_element_type=F32)
        m = _sigmoid(ga_ref[rows, :].astype(F32)) * ya + _sigmoid(gbr_ref[rows, :].astype(F32)) * yb
        x1 = x_ref[rows, :] + jnp.dot(m.astype(BF16), wo_ref[...], preferred_element_type=F32)
        x1_ref[rows, :] = x1
        h = x1 * lax.rsqrt(jnp.mean(x1 * x1, axis=-1, keepdims=True) + EPS) * g_ref[...]
        hh = h.astype(BF16)
        h_ref[rows, :] = hh
        hl = (h - hh.astype(F32)).astype(BF16)
        lg_ref[rows, :] = (jnp.dot(hh, wrh_ref[...], preferred_element_type=F32)
                           + jnp.dot(hl, wrh_ref[...], preferred_element_type=F32)
                           + jnp.dot(hh, wrl_ref[...], preferred_element_type=F32))


def _merge(ua2, zb2, proj2, gate_col0, x2, wa, wb, wo, g_ffn, w_router, wcast, tm=512, epad=128):
    T, D = x2.shape
    crow = wcast.shape[0] // (T // tm)
    gc = gate_col0 // D
    E = w_router.shape[1]
    wr = jnp.pad(w_router, ((0, 0), (0, epad - E)))
    wrh = wr.astype(BF16)
    wrl = (wr - wrh.astype(F32)).astype(BF16)
    row = lambda i: (i, 0)
    const = lambda *shape: pl.BlockSpec(shape, lambda i: (0,) * len(shape))
    return pl.pallas_call(
        _merge_kernel,
        grid=(T // tm,),
        in_specs=[
            pl.BlockSpec((tm, D), row), pl.BlockSpec((tm, D), row),
            pl.BlockSpec((tm, D), lambda i: (i, gc)), pl.BlockSpec((tm, D), lambda i: (i, gc + 1)),
            pl.BlockSpec((tm, D), row),
            const(D, D), const(D, D), const(D, D), const(1, D), const(D, epad), const(D, epad),
            pl.BlockSpec((crow, wcast.shape[1]), row),
        ],
        out_specs=[pl.BlockSpec((tm, D), row), pl.BlockSpec((tm, D), row), pl.BlockSpec((tm, epad), row),
                   pl.BlockSpec((crow, wcast.shape[1]), row)],
        out_shape=[jax.ShapeDtypeStruct((T, D), F32), jax.ShapeDtypeStruct((T, D), BF16),
                   jax.ShapeDtypeStruct((T, epad), F32), jax.ShapeDtypeStruct(wcast.shape, BF16)],
        compiler_params=_params(("parallel",)),
        name="merge",
    )(ua2, zb2, proj2, proj2, x2, wa, wb, wo, g_ffn.reshape(1, D), wrh, wrl, wcast)


TOKEN_BLOCK = 512


def _route_kernel(lg_ref, tri_ref, lmat_ref, affrow_ref, srow_ref, scol_ref, cnt_ref, *, n_exp, cap):
    S, EP = lg_ref.shape
    nchunk = S // 128
    lane = lax.broadcasted_iota(jnp.int32, (S, EP), 1)
    lg = jnp.where(lane < n_exp, lg_ref[...], -1e30)
    e = jnp.exp(lg - jnp.max(lg, axis=-1, keepdims=True))
    aff = e / jnp.sum(e, axis=-1, keepdims=True)
    rows = [aff[c * 128:(c + 1) * 128, :].T[:n_exp, :] for c in range(nchunk)]
    ar = jnp.concatenate(rows, axis=1)
    affrow_ref[...] = ar

    def count_ge(thr):
        return jnp.sum(jnp.where(ar >= thr, 1.0, 0.0), axis=-1, keepdims=True)

    def cond(state):
        lo, hi, it = state
        mid = 0.5 * (lo + hi)
        open_ = jnp.logical_and(mid != lo, mid != hi)
        return jnp.logical_and(jnp.max(jnp.where(open_, 1.0, 0.0)) > 0.0, it < 400)

    def body(state):
        lo, hi, it = state
        mid = 0.5 * (lo + hi)
        ok = count_ge(mid) >= cap
        return jnp.where(ok, mid, lo), jnp.where(ok, hi, mid), it + 1

    lo0 = jnp.zeros((n_exp, 1), F32)
    hi0 = jnp.full((n_exp, 1), 2.0, F32)
    thr, _, _ = lax.while_loop(cond, body, (lo0, hi0, jnp.int32(0)))

    def cumsum_excl(x):
        xs = jnp.concatenate([x[:, c * 128:(c + 1) * 128] for c in range(nchunk)], axis=0)
        cs = jnp.dot(xs.astype(BF16), tri_ref[...], preferred_element_type=F32)
        off = jnp.dot(lmat_ref[...], cs.astype(BF16), preferred_element_type=F32)[:, 127:128]
        inc = cs + off
        return jnp.concatenate([inc[c * n_exp:(c + 1) * n_exp, :] for c in range(nchunk)], axis=1) - x

    gt = ar > thr
    eq = ar == thr
    need = cap - jnp.sum(jnp.where(gt, 1.0, 0.0), axis=-1, keepdims=True)
    tie_rank = cumsum_excl(jnp.where(eq, 1.0, 0.0))
    sel = jnp.logical_or(gt, jnp.logical_and(eq, tie_rank < need))
    self_ = jnp.where(sel, 1.0, 0.0)
    pos = cumsum_excl(self_)
    slot = jnp.where(sel, pos, -1.0)
    srow_ref[...] = slot.astype(jnp.int32)
    clane = lax.broadcasted_iota(jnp.int32, (n_exp, 128), 1)
    cnt = jnp.where(clane == S // TOKEN_BLOCK, float(cap), 0.0)
    for k in range(S // TOKEN_BLOCK):
        cnt = jnp.where(clane == k, pos[:, k * TOKEN_BLOCK:k * TOKEN_BLOCK + 1], cnt)
    cnt_ref[...] = cnt.astype(jnp.int32)
    slot_p = jnp.concatenate([slot, jnp.full((EP - n_exp, S), -1.0, F32)], axis=0)
    cols = [slot_p[:, c * 128:(c + 1) * 128].T for c in range(nchunk)]
    scol_ref[...] = jnp.concatenate(cols, axis=0).astype(jnp.int32)


def _route(logits3, n_exp, cap):
    B, S, EP = logits3.shape
    nchunk = S // 128
    i = np.arange(128)
    tri = (i[:, None] <= i[None, :]).astype(np.float32)
    r = np.arange(nchunk * n_exp)
    lmat = ((r[:, None] % n_exp == r[None, :] % n_exp) & (r[None, :] // n_exp < r[:, None] // n_exp)).astype(np.float32)
    const = lambda *shape: pl.BlockSpec(shape, lambda b: (0,) * len(shape))
    return pl.pallas_call(
        functools.partial(_route_kernel, n_exp=n_exp, cap=cap),
        grid=(B,),
        in_specs=[pl.BlockSpec((None, S, EP), lambda b: (b, 0, 0)), const(128, 128), const(*lmat.shape)],
        out_specs=[
            pl.BlockSpec((None, n_exp, S), lambda b: (b, 0, 0)),
            pl.BlockSpec((None, n_exp, S), lambda b: (b, 0, 0)),
            pl.BlockSpec((None, S, EP), lambda b: (b, 0, 0)),
            pl.BlockSpec((None, n_exp, 128), lambda b: (b, 0, 0)),
        ],
        out_shape=[
            jax.ShapeDtypeStruct((B, n_exp, S), F32),
            jax.ShapeDtypeStruct((B, n_exp, S), jnp.int32), jax.ShapeDtypeStruct((B, S, EP), jnp.int32),
            jax.ShapeDtypeStruct((B, n_exp, 128), jnp.int32),
        ],
        compiler_params=_params(("parallel",)),
        name="route",
    )(logits3, jnp.asarray(tri, BF16), jnp.asarray(lmat, BF16))


GATHER_ROWS = 128


def _gather_kernel(cnt_ref, h_ref, srow_ref, affrow_ref, wc_ref, xe_ref, val_ref, wco_ref, acc_ref, vacc_ref, *, cap):
    b = pl.program_id(0)
    e = pl.program_id(1)
    S = h_ref.shape[0]
    nblk = S // TOKEN_BLOCK
    base = (b * pl.num_programs(1) + e) * (nblk + 1)
    R = GATHER_ROWS
    wco_ref[...] = wc_ref[...].astype(BF16)

    acc_ref[...] = jnp.zeros_like(acc_ref)
    vacc_ref[...] = jnp.zeros_like(vacc_ref)
    row = lax.broadcasted_iota(jnp.int32, (R, TOKEN_BLOCK), 0)

    def window(k, w0, first):
        tok = slice(k * TOKEN_BLOCK, (k + 1) * TOKEN_BLOCK)
        idx = row + w0
        hit = idx == srow_ref[pl.ds(e, 1), tok]
        if first is not None:
            hit = jnp.logical_and(hit, idx >= first)
        rows = pl.ds(pl.multiple_of(w0, F32_ROWS), R)
        acc_ref[rows, :] += jnp.dot(jnp.where(hit, 1.0, 0.0).astype(BF16), h_ref[tok, :],
                                    preferred_element_type=F32)
        vacc_ref[rows, :] += jnp.sum(jnp.where(hit, affrow_ref[pl.ds(e, 1), tok], 0.0), axis=-1, keepdims=True)

    w0 = [jnp.minimum((cnt_ref[base + k] // F32_ROWS) * F32_ROWS, cap - R) for k in range(nblk)]
    for k in range(nblk):
        window(k, w0[k], None)
    for k in range(nblk):
        def more(w, k=k):
            window(k, jnp.minimum(w, cap - R), w)
            return w + R

        lax.while_loop(lambda w, hi=cnt_ref[base + k + 1]: w < hi, more, w0[k] + R)

    xe_ref[...] = acc_ref[...].astype(xe_ref.dtype)
    val_ref[...] = vacc_ref[...]


def _gather(h3, srow, affrow, cnt, cap, wcast):
    B, S, D = h3.shape
    E = srow.shape[1]
    crow = wcast.shape[0] // (B * E)
    step = lambda b, e, cnt: (b * E + e, 0)
    return pl.pallas_call(
        functools.partial(_gather_kernel, cap=cap),
        grid_spec=pltpu.PrefetchScalarGridSpec(
            num_scalar_prefetch=1,
            grid=(B, E),
            in_specs=[
                pl.BlockSpec((None, S, D), lambda b, e, cnt: (b, 0, 0)),
                pl.BlockSpec((None, E, S), lambda b, e, cnt: (b, 0, 0)),
                pl.BlockSpec((None, E, S), lambda b, e, cnt: (b, 0, 0)),
                pl.BlockSpec((crow, wcast.shape[1]), step),
            ],
            out_specs=[
                pl.BlockSpec((None, None, cap, D), lambda b, e, cnt: (b, e, 0, 0)),
                pl.BlockSpec((None, None, cap, 1), lambda b, e, cnt: (b, e, 0, 0)),
                pl.BlockSpec((crow, wcast.shape[1]), step),
            ],
            scratch_shapes=[pltpu.VMEM((cap, D), F32), pltpu.VMEM((cap, 1), F32)],
        ),
        out_shape=[
            jax.ShapeDtypeStruct((B, E, cap, D), BF16), jax.ShapeDtypeStruct((B, E, cap, 1), F32),
            jax.ShapeDtypeStruct(wcast.shape, BF16),
        ],
        compiler_params=_params(("arbitrary", "arbitrary")),
        name="gather",
    )(cnt, h3, srow, affrow, wcast)


def _ffn_kernel(xe_ref, val_ref, wg_ref, wu_ref, wd_ref, ye_ref):
    xe = xe_ref[...]
    gt = jnp.dot(xe, wg_ref[...], preferred_element_type=F32)
    up = jnp.dot(xe, wu_ref[...], preferred_element_type=F32)
    act = (gt * _sigmoid(gt) * up).astype(BF16)
    ye = jnp.dot(act, wd_ref[...], preferred_element_type=F32)
    ye_ref[...] = (ye * val_ref[...]).astype(ye_ref.dtype)


def _ffn(xe, vals, wg, wu, wd):
    B, E, C, D = xe.shape
    F = wg.shape[2]
    return pl.pallas_call(
        _ffn_kernel,
        grid=(E, B),
        in_specs=[
            pl.BlockSpec((None, None, C, D), lambda e, b: (b, e, 0, 0)),
            pl.BlockSpec((None, None, C, 1), lambda e, b: (b, e, 0, 0)),
            pl.BlockSpec((None, D, F), lambda e, b: (e, 0, 0)),
            pl.BlockSpec((None, D, F), lambda e, b: (e, 0, 0)),
            pl.BlockSpec((None, F, D), lambda e, b: (e, 0, 0)),
        ],
        out_specs=pl.BlockSpec((None, None, C, D), lambda e, b: (b, e, 0, 0)),
        out_shape=jax.ShapeDtypeStruct((B, E, C, D), BF16),
        compiler_params=_params(("parallel", "arbitrary")),
        name="ffn",
    )(xe, vals, wg, wu, wd)


COMBINE_SLOTS = 256


def _combine_kernel(cnt_ref, ye_ref, scol_ref, x1_ref, g_ref, o_ref, extra_ref, *, n_exp, cap):
    b = pl.program_id(0)
    r = pl.program_id(1)
    nblk1 = pl.num_programs(1) + 1
    tm = x1_ref.shape[0]
    KW = COMBINE_SLOTS
    lane = lax.broadcasted_iota(jnp.int32, (tm, KW), 1)
    scol = scol_ref[...]
    extra_ref[...] = jnp.zeros_like(extra_ref)

    def window(e, w0, first):
        idx = lane + w0
        hit = idx == scol[:, e:e + 1]
        if first is not None:
            hit = jnp.logical_and(hit, idx >= first)
        ye = ye_ref[e, pl.ds(pl.multiple_of(w0, BF16_ROWS), KW), :]
        return jnp.dot(jnp.where(hit, 1.0, 0.0).astype(BF16), ye, preferred_element_type=F32)

    base = [(b * n_exp + e) * nblk1 + r for e in range(n_exp)]
    w0 = [jnp.minimum((cnt_ref[base[e]] // BF16_ROWS) * BF16_ROWS, cap - KW) for e in range(n_exp)]
    acc = x1_ref[...]
    for e in range(n_exp):
        acc = acc + window(e, w0[e], None)
    for e in range(n_exp):
        def more(w, e=e):
            extra_ref[...] += window(e, jnp.minimum(w, cap - KW), w)
            return w + KW

        lax.while_loop(lambda w, hi=cnt_ref[base[e] + 1]: w < hi, more, w0[e] + KW)

    acc = acc + extra_ref[...]
    y = acc * lax.rsqrt(jnp.mean(acc * acc, axis=-1, keepdims=True) + EPS)
    o_ref[...] = y * g_ref[...]


def _combine(ye, scol, cnt, x13, g_final):
    B, E, C, D = ye.shape
    S = x13.shape[1]
    EP = scol.shape[2]
    tm = TOKEN_BLOCK
    return pl.pallas_call(
        functools.partial(_combine_kernel, n_exp=E, cap=C),
        grid_spec=pltpu.PrefetchScalarGridSpec(
            num_scalar_prefetch=1,
            grid=(B, S // tm),
            in_specs=[
                pl.BlockSpec((None, E, C, D), lambda b, r, cnt: (b, 0, 0, 0)),
                pl.BlockSpec((None, tm, EP), lambda b, r, cnt: (b, r, 0)),
                pl.BlockSpec((None, tm, D), lambda b, r, cnt: (b, r, 0)),
                pl.BlockSpec((1, D), lambda b, r, cnt: (0, 0)),
            ],
            out_specs=pl.BlockSpec((None, tm, D), lambda b, r, cnt: (b, r, 0)),
            scratch_shapes=[pltpu.VMEM((tm, D), F32)],
        ),
        out_shape=jax.ShapeDtypeStruct((B, S, D), F32),
        compiler_params=_params(("arbitrary", "arbitrary")),
        name="combine",
    )(cnt, ye, scol, x13, g_final.reshape(1, D))


def kernel(x, g_mix, w_in, conv_a_w, conv_a_b, lru_w_r, lru_b_r, lru_w_i, lru_b_i, lru_lambda, w_a_out, conv_b_w, conv_b_b, filt_w1, filt_b1, filt_w2, filt_b2, filt_w3, filt_freq, filt_bias, w_b_out, w_o, g_ffn, w_router, w_gate, w_up, w_down, g_final):
    B, S, D = x.shape
    assert w_in.shape[0] == 1, "single-layer block only"
    l = 0
    lru_w = conv_a_w.shape[2]
    hy_w = filt_bias.shape[2]
    n_exp = w_router.shape[2]
    cap = CAPACITY_FACTOR * S // n_exp
    assert 2 * S == N_FFT
    ma32, ma64, gb, gbi, mai = (jnp.asarray(c).astype(BF16) for c in _dft_constants())

    x2 = x.reshape(B * S, D)
    proj2 = _inproj(x2, g_mix[l], w_in[l])
    proj3 = proj2.reshape(B, S, -1)
    n_ff = w_gate.shape[3]
    ua, wu = _rglru(proj3, conv_a_w[l], conv_a_b[l], lru_w_r[l], lru_b_r[l], lru_w_i[l], lru_b_i[l], lru_lambda[l],
                    w_up[l].reshape(n_exp * D, n_ff))
    kf = _filter_spectra(S, filt_w1[l], filt_b1[l], filt_w2[l], filt_b2[l], filt_w3[l], filt_freq[l], hy_w, ma64, gb)
    zb = _hyena(proj3, 2 * lru_w, conv_b_w[l], conv_b_b[l], filt_bias[l], kf, ma32, gb, gbi, mai)
    x1, h, logits, wd = _merge(ua.reshape(B * S, lru_w), zb.reshape(B * S, hy_w), proj2, 2 * lru_w + 3 * hy_w, x2,
                               w_a_out[l].astype(BF16), w_b_out[l].astype(BF16), w_o[l].astype(BF16), g_ffn[l],
                               w_router[l], w_down[l].reshape(n_exp * n_ff, D))
    affrow, srow, scol, cnt = _route(logits.reshape(B, S, -1), n_exp, cap)
    cnt = cnt[:, :, :S // TOKEN_BLOCK + 1].reshape(-1)
    xe, vals, wg = _gather(h.reshape(B, S, D), srow, affrow, cnt, cap, w_gate[l].reshape(n_exp * D, n_ff))
    ye = _ffn(xe, vals, wg.reshape(n_exp, D, n_ff), wu.reshape(n_exp, D, n_ff), wd.reshape(n_exp, n_ff, D))
    return _combine(ye, scol, cnt, x1.reshape(B, S, D), g_final)
```

```python
import functools
import math

import numpy as np
import jax
import jax.numpy as jnp
from jax import lax
from jax.experimental import pallas as pl
from jax.experimental.pallas import tpu as pltpu

F32 = jnp.float32
BF16 = jnp.bfloat16
EPS = 1e-6
LOG2E = 1.4426950408889634

VMEM_LIMIT_BYTES = 56 * 1024 * 1024
BF16_ROWS = 16
F32_ROWS = 8
LANES = 128

LRU_HEADS = 16
LRU_C = 8.0
CAPACITY_FACTOR = 2
FILTER_BANDS = 16
DECAY_FAST = 0.3
DECAY_SLOW = 1.5
DECAY_TARGET = 1e-2

FFT_N1 = 64
FFT_N2 = 128
N_FFT = FFT_N1 * FFT_N2
FFT_KH = FFT_N1 // 2 + 1
FFT_G = F32_ROWS


def _params(sem):
    return pltpu.CompilerParams(dimension_semantics=sem, vmem_limit_bytes=VMEM_LIMIT_BYTES)


def _inproj_kernel(x_ref, g_ref, w_ref, o_ref, h_ref, wb_ref):
    j = pl.program_id(1)

    @pl.when(pl.program_id(0) == 0)
    def _():
        wb_ref[j] = w_ref[...].astype(BF16)

    @pl.when(j == 0)
    def _():
        x = x_ref[...]
        y = x * lax.rsqrt(jnp.mean(x * x, axis=-1, keepdims=True) + EPS)
        h_ref[...] = (y * g_ref[...]).astype(BF16)

    o_ref[...] = jnp.dot(h_ref[...], wb_ref[j], preferred_element_type=F32).astype(o_ref.dtype)


def _inproj(x2, g, w, tm=1024, tn=1792):
    T, D = x2.shape
    N = w.shape[1]
    nj = N // tn
    return pl.pallas_call(
        _inproj_kernel,
        grid=(T // tm, nj),
        in_specs=[
            pl.BlockSpec((tm, D), lambda i, j: (i, 0)),
            pl.BlockSpec((1, D), lambda i, j: (0, 0)),
            pl.BlockSpec((D, tn), lambda i, j: (0, jnp.where(i == 0, j, nj - 1)), pipeline_mode=pl.Buffered(1)),
        ],
        out_specs=pl.BlockSpec((tm, tn), lambda i, j: (i, j)),
        out_shape=jax.ShapeDtypeStruct((T, N), BF16),
        scratch_shapes=[pltpu.VMEM((tm, D), BF16), pltpu.VMEM((nj, D, tn), BF16)],
        compiler_params=_params(("arbitrary", "arbitrary")),
        name="inproj",
    )(x2, g.reshape(1, D), w)


PAD = BF16_ROWS


def _stage_padded(src_ref, pad_ref):
    S, W = src_ref.shape
    pad_ref[0:PAD, :] = jnp.zeros((PAD, W), pad_ref.dtype)
    pad_ref[PAD + S:PAD + S + PAD, :] = jnp.zeros((PAD, W), pad_ref.dtype)
    pad_ref[PAD:PAD + S, :] = src_ref[...]


def _conv_chunk(pad_ref, r0, rows, taps, w_ref, b_ref):
    n = rows + 2 * PAD
    win = pad_ref[pl.ds(r0, n), :].astype(F32)
    acc = None
    for k, off in enumerate(taps):
        sh = win if off == 0 else pltpu.roll(win, (-off) % n, 0)
        term = sh[PAD:PAD + rows, :] * w_ref[k:k + 1, :]
        acc = term if acc is None else acc + term
    return acc + b_ref[...]


NSEG = F32_ROWS
SCAN_UNROLL = 8


def _sigmoid(x):
    return 1.0 / (1.0 + jnp.exp2(x * (-LOG2E)))


def _gelu_tanh(x):
    return 0.5 * x * (1.0 + jnp.tanh(math.sqrt(2.0 / math.pi) * (x + 0.044715 * (x * x * x))))


def _softplus(x):
    return jnp.maximum(x, 0.0) + jnp.log(1.0 + jnp.exp(-jnp.abs(x)))


def _ld(ref, rows):
    return jnp.concatenate([ref[l, rows, :] for l in range(ref.shape[0])], axis=1)


def _st(ref, rows, v):
    for l in range(ref.shape[0]):
        ref[l, rows, :] = v[:, l * LANES:(l + 1) * LANES]


def _rglru_kernel(ax_ref, gate_ref, cw_ref, cb_ref, wcat_ref, bias_ref, lam_ref, wc_ref, o_ref, wco_ref,
                  pad_ref, af_ref, bf_ref, ab_ref, bb_ref, hs_ref):
    wco_ref[...] = wc_ref[...].astype(BF16)
    S, W = ax_ref.shape
    seg = S // NSEG
    pitch = hs_ref.shape[1] // NSEG
    _stage_padded(ax_ref, pad_ref)
    hc = [(0.5 * LRU_C * LOG2E) * _softplus(-lam_ref[d:d + 1, :]) for d in range(2)]

    def gates(c, _):
        r0 = pl.multiple_of(c * seg, seg)
        xa = _conv_chunk(pad_ref, r0, seg, (-2, -1, 0, 1), cw_ref, cb_ref)
        pre = jnp.dot(xa.astype(BF16), wcat_ref[...], preferred_element_type=F32)
        hx = 0.5 * xa
        pos = r0 + lax.broadcasted_iota(jnp.int32, (seg, W), 0)
        for d, (a_ref, b_ref, start) in enumerate(((af_ref, bf_ref, 0), (ab_ref, bb_ref, S - 1))):
            tr = jnp.tanh(pre[:, (2 * d) * W:(2 * d + 1) * W] + bias_ref[2 * d:2 * d + 1, :])
            ti = jnp.tanh(pre[:, (2 * d + 1) * W:(2 * d + 2) * W] + bias_ref[2 * d + 1:2 * d + 2, :])
            a = jnp.exp2(-hc[d] - hc[d] * tr)
            m2 = 1.0 - a * a
            mult = jnp.where(m2 > 0.0, m2 * lax.rsqrt(m2), 0.0)
            mult = jnp.where(pos == start, 1.0, mult)
            _st(a_ref, pl.ds(c, seg, stride=NSEG), a)
            _st(b_ref, pl.ds(c, seg, stride=NSEG), mult * (hx + hx * ti))
        return 0

    lax.fori_loop(0, NSEG, gates, 0)

    def scan(u, carry):
        hf, pf, hb, pb = carry
        for k in range(SCAN_UNROLL):
            j = u * SCAN_UNROLL + k
            rf = pl.multiple_of(j * NSEG, NSEG)
            rb = pl.multiple_of((seg - 1 - j) * NSEG, NSEG)
            a = _ld(af_ref, pl.ds(rf, NSEG))
            hf = a * hf + _ld(bf_ref, pl.ds(rf, NSEG))
            pf = a * pf
            _st(bf_ref, pl.ds(rf, NSEG), hf)
            _st(af_ref, pl.ds(rf, NSEG), pf)
            a = _ld(ab_ref, pl.ds(rb, NSEG))
            hb = a * hb + _ld(bb_ref, pl.ds(rb, NSEG))
            pb = a * pb
            _st(bb_ref, pl.ds(rb, NSEG), hb)
            _st(ab_ref, pl.ds(rb, NSEG), pb)
        return hf, pf, hb, pb

    zero = jnp.zeros((NSEG, W), F32)
    one = jnp.ones((NSEG, W), F32)
    hf, pf, hb, pb = lax.fori_loop(0, seg // SCAN_UNROLL, scan, (zero, one, zero, one))

    row = lax.broadcasted_iota(jnp.int32, (NSEG, W), 0)
    cf = zero
    cb = zero
    for _ in range(NSEG - 1):
        cf = jnp.where(row >= 1, pltpu.roll(hf + pf * cf, 1, 0), 0.0)
        cb = jnp.where(row < NSEG - 1, pltpu.roll(hb + pb * cb, NSEG - 1, 0), 0.0)

    def fix(u, _):
        for k in range(SCAN_UNROLL):
            j = u * SCAN_UNROLL + k
            r = pl.multiple_of(j * NSEG, NSEG)
            rows = pl.ds(r, NSEG)
            h = _ld(bf_ref, rows) + _ld(af_ref, rows) * cf + _ld(bb_ref, rows) + _ld(ab_ref, rows) * cb
            _st(hs_ref, pl.ds(j, NSEG, stride=pitch), h)
        return 0

    lax.fori_loop(0, seg // SCAN_UNROLL, fix, 0)

    def out(c, _):
        r0 = pl.multiple_of(c * seg, seg)
        h = _ld(hs_ref, pl.ds(pl.multiple_of(c * pitch, F32_ROWS), seg))
        g = gate_ref[pl.ds(r0, seg), :].astype(F32)
        o_ref[pl.ds(r0, seg), :] = (h * _gelu_tanh(g)).astype(o_ref.dtype)
        return 0

    lax.fori_loop(0, NSEG, out, 0)


def _rglru(proj3, conv_w, conv_b, w_r, b_r, w_i, b_i, lam, wcast, wt=256):
    B, S, _ = proj3.shape
    W = conv_w.shape[1]
    nj = W // wt
    crow = wcast.shape[0] // (B * nj)
    cstep = lambda b, j: (b * nj + j, 0)
    hd = W // LRU_HEADS
    hpt = wt // hd
    pitch = S // NSEG + F32_ROWS

    def blockdiag(w):
        w4 = w.reshape(nj, hpt, hd, hd)
        eye = jnp.eye(hpt, dtype=w.dtype)
        return jnp.einsum('jhab,hg->jhagb', w4, eye).reshape(nj, wt, wt)

    wcat = (0.5 * jnp.concatenate([blockdiag(w_r[0]), blockdiag(w_i[0]), blockdiag(w_r[1]), blockdiag(w_i[1])],
                                  axis=-1)).astype(BF16)
    bias = 0.5 * jnp.stack([b_r[0], b_i[0], b_r[1], b_i[1]], axis=0)
    gate_off = W // wt
    return pl.pallas_call(
        _rglru_kernel,
        grid=(B, nj),
        in_specs=[
            pl.BlockSpec((None, S, wt), lambda b, j: (b, 0, j)),
            pl.BlockSpec((None, S, wt), lambda b, j: (b, 0, gate_off + j)),
            pl.BlockSpec((conv_w.shape[0], wt), lambda b, j: (0, j)),
            pl.BlockSpec((1, wt), lambda b, j: (0, j)),
            pl.BlockSpec((None, wt, 4 * wt), lambda b, j: (j, 0, 0)),
            pl.BlockSpec((4, wt), lambda b, j: (0, j)),
            pl.BlockSpec((2, wt), lambda b, j: (0, j)),
            pl.BlockSpec((crow, wcast.shape[1]), cstep),
        ],
        out_specs=[pl.BlockSpec((None, S, wt), lambda b, j: (b, 0, j)), pl.BlockSpec((crow, wcast.shape[1]), cstep)],
        out_shape=[jax.ShapeDtypeStruct((B, S, W), BF16), jax.ShapeDtypeStruct(wcast.shape, BF16)],
        scratch_shapes=[pltpu.VMEM((S + 2 * PAD, wt), BF16)] + [pltpu.VMEM((wt // LANES, S, LANES), F32)] * 4
        + [pltpu.VMEM((wt // LANES, NSEG * pitch, LANES), F32)],
        compiler_params=_params(("parallel", "parallel")),
        name="rglru",
    )(proj3, proj3, conv_w, conv_b.reshape(1, W), wcat, bias, lam, wcast)


COMPS_FWD = [(c, k1) for c in range(2) for k1 in range(FFT_KH)]
COMPS_INV = [(c, k1) for (c, k1) in COMPS_FWD if not (c == 1 and k1 in (0, FFT_KH - 1))]


@functools.lru_cache(maxsize=None)
def _dft_constants():
    g = FFT_G
    eye = np.eye(g)

    def stage_a(n1_in):
        k1 = np.arange(FFT_KH)[:, None]
        n1 = np.arange(n1_in)[None, :]
        ang = -2.0 * np.pi * ((k1 * n1) % FFT_N1) / FFT_N1
        a = np.stack([np.cos(ang), np.sin(ang)])
        a = np.stack([a[c, k] for (c, k) in COMPS_FWD])
        return np.einsum('qn,st->qsnt', a, eye).reshape(len(COMPS_FWD) * g, n1_in * g)

    k1 = np.arange(FFT_KH)[:, None, None]
    k2 = np.arange(FFT_N2)[None, :, None]
    n2 = np.arange(FFT_N2)[None, None, :]
    ang = -2.0 * np.pi * ((n2 * (k1 + FFT_N1 * k2)) % N_FFT) / N_FFT
    gr, gi = np.cos(ang), np.sin(ang)
    gb = np.concatenate([np.concatenate([gr, -gi], axis=2), np.concatenate([gi, gr], axis=2)], axis=1)
    gbi = np.transpose(gb, (0, 2, 1))

    c = np.full(FFT_KH, 2.0)
    c[0] = 1.0
    c[-1] = 1.0
    n1 = np.arange(FFT_N1 // 2)[:, None]
    kk = np.arange(FFT_KH)[None, :]
    ang = 2.0 * np.pi * ((n1 * kk) % FFT_N1) / FFT_N1
    bi = np.stack([np.cos(ang) * c / N_FFT, -np.sin(ang) * c / N_FFT])
    bi = np.stack([bi[c_, :, k] for (c_, k) in COMPS_INV], axis=1)
    mai = np.einsum('nq,st->nsqt', bi, eye).reshape((FFT_N1 // 2) * g, len(COMPS_INV) * g)
    f = np.float32
    return stage_a(FFT_N1 // 2).astype(f), stage_a(FFT_N1).astype(f), gb.astype(f), gbi.astype(f), mai.astype(f)


def _split_rows(tiles):
    f = [t.astype(F32) for t in tiles]
    return [jnp.concatenate([t[h * FFT_G:(h + 1) * FFT_G] for t in f], axis=0).astype(BF16)
            for h in range(BF16_ROWS // FFT_G)]


def _join_rows(outs, q):
    return jnp.concatenate([o[q * FFT_G:(q + 1) * FFT_G] for o in outs], axis=0)


def _stage_a_fwd(src_ref, ma_ref, y_ref, n1_in):
    for p in range(FFT_N2 // BF16_ROWS):
        r = BF16_ROWS * p
        opnds = _split_rows([src_ref[FFT_N2 * n1 + r:FFT_N2 * n1 + r + BF16_ROWS, :] for n1 in range(n1_in)])
        outs = [jnp.dot(ma_ref[...], o, preferred_element_type=F32) for o in opnds]
        for q, (c, k1) in enumerate(COMPS_FWD):
            y_ref[k1, c * FFT_N2 + r:c * FFT_N2 + r + BF16_ROWS, :] = _join_rows(outs, q).astype(y_ref.dtype)


def _stage_a_inv(y_ref, mai_ref, p):
    r = BF16_ROWS * p
    opnds = _split_rows([y_ref[k1, c * FFT_N2 + r:c * FFT_N2 + r + BF16_ROWS, :] for (c, k1) in COMPS_INV])
    outs = [jnp.dot(mai_ref[...], o, preferred_element_type=F32) for o in opnds]
    return [_join_rows(outs, n1) for n1 in range(FFT_N1 // 2)]


def _stage_b_fwd_slab(gb_ref, y_ref, k1):
    return jnp.dot(gb_ref[k1], y_ref[k1], preferred_element_type=F32)


def _dot3(a, b):
    ah = a.astype(BF16)
    al = (a - ah.astype(F32)).astype(BF16)
    bh = b.astype(BF16)
    bl = (b - bh.astype(F32)).astype(BF16)
    d = lambda u, v: jnp.dot(u, v, preferred_element_type=F32)
    return d(ah, bh) + d(al, bh) + d(ah, bl)


def _filtmlp_kernel(fa_ref, fb_ref, w1_ref, b1_ref, w2_ref, b2_ref, freq_ref, o_ref):
    u = jnp.concatenate([_dot3(fa_ref[...], w1_ref[...]), _dot3(fb_ref[...], w1_ref[...])], axis=1)
    h = jnp.sin(freq_ref[0:1, :] * (u + b1_ref[...]))
    o_ref[...] = jnp.sin(freq_ref[1:2, :] * (_dot3(h, w2_ref[...]) + b2_ref[...]))


def _filtfft_kernel(h_ref, t_ref, w3f_ref, w3b_ref, delta_ref, ma_ref, gb_ref, o_ref, kc_ref, y_ref, *, chunk):
    n, W = kc_ref.shape
    L = n // 2

    def body(c, _):
        r0 = pl.multiple_of(c * chunk, chunk)
        h = h_ref[pl.ds(r0, chunk), :].astype(BF16)
        kf = jnp.dot(h, w3f_ref[...], preferred_element_type=F32)
        kb = jnp.dot(h, w3b_ref[...], preferred_element_type=F32)
        row = r0 + lax.broadcasted_iota(jnp.int32, (chunk, W), 0)
        kb = jnp.where(row == 0, 0.0, kb)
        kc_ref[pl.ds(r0, chunk), :] = (kf * jnp.exp(-t_ref[pl.ds(r0, chunk), :] * delta_ref[...])).astype(kc_ref.dtype)
        kc_ref[pl.ds(L + r0, chunk), :] = (kb * jnp.exp(-t_ref[pl.ds(L + r0, chunk), :] * delta_ref[...])).astype(kc_ref.dtype)
        return 0

    lax.fori_loop(0, L // chunk, body, 0)
    _stage_a_fwd(kc_ref, ma_ref, y_ref, FFT_N1)

    def slab(k1, _):
        o_ref[k1] = _stage_b_fwd_slab(gb_ref, y_ref, k1).astype(o_ref.dtype)
        return 0

    lax.fori_loop(0, FFT_KH, slab, 0, unroll=11)


def _filter_spectra(L, w1, b1, w2, b2, w3, freq, W, ma64, gb, wt=256, chunk=512):
    f32 = F32
    order = w3.shape[1] // (2 * W)
    hid = w1.shape[1]
    t = jnp.linspace(0.0, 1.0, L, dtype=f32)[:, None]
    w = (2.0 * math.pi / L) * jnp.arange(L, dtype=f32)[:, None]
    f = jnp.linspace(1e-4, FILTER_BANDS - 1, FILTER_BANDS, dtype=f32)[None, :]
    z = jnp.concatenate([t, jnp.cos(w * f), -jnp.sin(w * f)], axis=-1)
    emb = z.shape[1]
    embp = 128
    z = jnp.pad(z, ((0, 0), (0, embp - emb)))
    feat = jnp.concatenate([z, z[:1], z[:0:-1]], axis=0)
    tt = jnp.concatenate([t, t[:1], t[:0:-1]], axis=0)
    w1p = jnp.pad(w1, ((0, embp - emb), (0, 0)))
    zero = jnp.zeros_like(w2)
    w2d = jnp.concatenate([jnp.concatenate([w2, zero], axis=1), jnp.concatenate([zero, w2], axis=1)], axis=0)
    two = lambda v: jnp.concatenate([v, v], axis=-1)
    nblk = L // chunk
    const1 = lambda *shape: pl.BlockSpec(shape, lambda i: (0,) * len(shape))
    hfeat = pl.pallas_call(
        _filtmlp_kernel,
        grid=(nblk,),
        in_specs=[
            pl.BlockSpec((chunk, embp), lambda i: (i, 0)), pl.BlockSpec((chunk, embp), lambda i: (nblk + i, 0)),
            const1(embp, hid), const1(1, 2 * hid), const1(2 * hid, 2 * hid), const1(1, 2 * hid), const1(2, 2 * hid),
        ],
        out_specs=pl.BlockSpec((chunk, 2 * hid), lambda i: (i, 0)),
        out_shape=jax.ShapeDtypeStruct((L, 2 * hid), f32),
        compiler_params=_params(("parallel",)),
        name="filtmlp",
    )(feat, feat, w1p, two(b1.reshape(1, hid)), w2d, two(b2.reshape(1, hid)), two(freq))

    w3b16 = w3.astype(BF16)
    zero3 = jnp.zeros_like(w3b16)
    w3f = jnp.concatenate([w3b16, zero3], axis=0)
    w3b = jnp.concatenate([zero3, w3b16], axis=0)
    min_decay = math.log(DECAY_TARGET) / DECAY_SLOW
    max_decay = math.log(DECAY_TARGET) / DECAY_FAST
    deltas = jnp.abs(jnp.linspace(min_decay, max_decay, W, dtype=f32))[None, :]
    nj = W // wt
    n = 2 * L
    const = lambda *shape: pl.BlockSpec(shape, lambda o, j: (0,) * len(shape))
    return pl.pallas_call(
        functools.partial(_filtfft_kernel, chunk=chunk),
        grid=(order, nj),
        in_specs=[
            const(L, 2 * hid), const(n, 1),
            pl.BlockSpec((2 * hid, wt), lambda o, j: (0, (2 * o) * nj + j)),
            pl.BlockSpec((2 * hid, wt), lambda o, j: (0, (2 * o + 1) * nj + j)),
            pl.BlockSpec((1, wt), lambda o, j: (0, j)),
            const(*ma64.shape), const(*gb.shape),
        ],
        out_specs=pl.BlockSpec((None, None, FFT_KH, 2 * FFT_N2, wt), lambda o, j: (o, j, 0, 0, 0)),
        out_shape=jax.ShapeDtypeStruct((order, nj, FFT_KH, 2 * FFT_N2, wt), BF16),
        scratch_shapes=[pltpu.VMEM((n, wt), BF16), pltpu.VMEM((FFT_KH, 2 * FFT_N2, wt), BF16)],
        compiler_params=_params(("parallel", "parallel")),
        name="filtfft",
    )(hfeat, tt, w3f, w3b, deltas, ma64, gb)


def _hyena_kernel(v_ref, x1_ref, x2_ref, cw_ref, cb_ref, fb_ref, kf_ref, ma_ref, gb_ref, gbi_ref, mai_ref, o_ref,
                  pad_ref, z_ref, g1_ref, g2_ref, y_ref, *, chunk):
    S, W = v_ref.shape
    g = BF16_ROWS

    for idx, (src, dst) in enumerate(((v_ref, z_ref), (x1_ref, g1_ref), (x2_ref, g2_ref))):
        _stage_padded(src, pad_ref)

        def conv(c, _, idx=idx, dst=dst):
            r0 = pl.multiple_of(c * chunk, chunk)
            y = _conv_chunk(pad_ref, r0, chunk, (-1, 0, 1), cw_ref.at[idx], cb_ref.at[idx])
            dst[pl.ds(r0, chunk), :] = y.astype(dst.dtype)
            return 0

        lax.fori_loop(0, S // chunk, conv, 0)

    for order, (gate_ref, dst) in enumerate(((g1_ref, z_ref), (g2_ref, o_ref))):
        _stage_a_fwd(z_ref, ma_ref, y_ref, FFT_N1 // 2)

        def slab(k1, _, order=order):
            x = _stage_b_fwd_slab(gb_ref, y_ref, k1)
            kf = kf_ref[order, k1].astype(F32)
            xr, xi = x[:FFT_N2], x[FFT_N2:]
            kr, ki = kf[:FFT_N2], kf[FFT_N2:]
            p = jnp.concatenate([xr * kr - xi * ki, xr * ki + xi * kr], axis=0).astype(BF16)
            y_ref[k1] = jnp.dot(gbi_ref[k1], p, preferred_element_type=F32).astype(y_ref.dtype)
            return 0

        lax.fori_loop(0, FFT_KH, slab, 0, unroll=True)

        bias = fb_ref[order:order + 1, :]
        for p in range(FFT_N2 // g):
            for n1, y in enumerate(_stage_a_inv(y_ref, mai_ref, p)):
                t0 = FFT_N2 * n1 + g * p
                zt = z_ref[t0:t0 + g, :].astype(F32)
                gt = gate_ref[t0:t0 + g, :].astype(F32)
                dst[t0:t0 + g, :] = (gt * (y + bias * zt)).astype(dst.dtype)


def _hyena(proj3, col0, conv_w, conv_b, filt_bias, kf, ma32, gb, gbi, mai, wt=256, chunk=512):
    B, S, _ = proj3.shape
    W = filt_bias.shape[1]
    nj = W // wt
    c0 = col0 // wt
    cw = conv_w.reshape(conv_w.shape[0], 3, W).transpose(1, 0, 2)
    cb = conv_b.reshape(3, 1, W)
    const = lambda *shape: pl.BlockSpec(shape, lambda j, b: (0,) * len(shape), pipeline_mode=pl.Buffered(1))
    return pl.pallas_call(
        functools.partial(_hyena_kernel, chunk=chunk),
        grid=(nj, B),
        in_specs=[
            pl.BlockSpec((None, S, wt), lambda j, b: (b, 0, c0 + j)),
            pl.BlockSpec((None, S, wt), lambda j, b: (b, 0, c0 + nj + j)),
            pl.BlockSpec((None, S, wt), lambda j, b: (b, 0, c0 + 2 * nj + j)),
            pl.BlockSpec((3, conv_w.shape[0], wt), lambda j, b: (0, 0, j)),
            pl.BlockSpec((3, 1, wt), lambda j, b: (0, 0, j)),
            pl.BlockSpec((filt_bias.shape[0], wt), lambda j, b: (0, j)),
            pl.BlockSpec((kf.shape[0], None, FFT_KH, 2 * FFT_N2, wt), lambda j, b: (0, j, 0, 0, 0),
                         pipeline_mode=pl.Buffered(1)),
            const(*ma32.shape), const(*gb.shape), const(*gbi.shape), const(*mai.shape),
        ],
        out_specs=pl.BlockSpec((None, S, wt), lambda j, b: (b, 0, j)),
        out_shape=jax.ShapeDtypeStruct((B, S, W), BF16),
        scratch_shapes=[pltpu.VMEM((S + 2 * PAD, wt), BF16)] + [pltpu.VMEM((S, wt), BF16)] * 3
        + [pltpu.VMEM((FFT_KH, 2 * FFT_N2, wt), BF16)],
        compiler_params=_params(("parallel", "arbitrary")),
        name="hyena",
    )(proj3, proj3, proj3, cw, cb, filt_bias, kf, ma32, gb, gbi, mai)


MERGE_SUB = 2


def _merge_kernel(ua_ref, zb_ref, ga_ref, gbr_ref, x_ref, wa_ref, wb_ref, wo_ref, g_ref, wrh_ref, wrl_ref, wc_ref,
                  x1_ref, h_ref, lg_ref, wco_ref):
    wco_ref[...] = wc_ref[...].astype(BF16)
    sub = x_ref.shape[0] // MERGE_SUB
    for s in range(MERGE_SUB):
        rows = slice(s * sub, (s + 1) * sub)
        ya = jnp.dot(ua_ref[rows, :], wa_ref[...], preferred_element_type=F32)
        yb = jnp.dot(zb_ref[rows, :], wb_ref[...], preferred_element_type=F32)
        m = _sigmoid(ga_ref[rows, :].astype(F32)) * ya + _sigmoid(gbr_ref[rows, :].astype(F32)) * yb
        x1 = x_ref[rows, :] + jnp.dot(m.astype(BF16), wo_ref[...], preferred_element_type=F32)
        x1_ref[rows, :] = x1
        h = x1 * lax.rsqrt(jnp.mean(x1 * x1, axis=-1, keepdims=True) + EPS) * g_ref[...]
        hh = h.astype(BF16)
        h_ref[rows, :] = hh
        hl = (h - hh.astype(F32)).astype(BF16)
        lg_ref[rows, :] = (jnp.dot(hh, wrh_ref[...], preferred_element_type=F32)
                           + jnp.dot(hl, wrh_ref[...], preferred_element_type=F32)
                           + jnp.dot(hh, wrl_ref[...], preferred_element_type=F32))


def _merge(ua2, zb2, proj2, gate_col0, x2, wa, wb, wo, g_ffn, w_router, wcast, tm=512, epad=128):
    T, D = x2.shape
    crow = wcast.shape[0] // (T // tm)
    gc = gate_col0 // D
    E = w_router.shape[1]
    wr = jnp.pad(w_router, ((0, 0), (0, epad - E)))
    wrh = wr.astype(BF16)
    wrl = (wr - wrh.astype(F32)).astype(BF16)
    row = lambda i: (i, 0)
    const = lambda *shape: pl.BlockSpec(shape, lambda i: (0,) * len(shape))
    return pl.pallas_call(
        _merge_kernel,
        grid=(T // tm,),
        in_specs=[
            pl.BlockSpec((tm, D), row), pl.BlockSpec((tm, D), row),
            pl.BlockSpec((tm, D), lambda i: (i, gc)), pl.BlockSpec((tm, D), lambda i: (i, gc + 1)),
            pl.BlockSpec((tm, D), row),
            const(D, D), const(D, D), const(D, D), const(1, D), const(D, epad), const(D, epad),
            pl.BlockSpec((crow, wcast.shape[1]), row),
        ],
        out_specs=[pl.BlockSpec((tm, D), row), pl.BlockSpec((tm, D), row), pl.BlockSpec((tm, epad), row),
                   pl.BlockSpec((crow, wcast.shape[1]), row)],
        out_shape=[jax.ShapeDtypeStruct((T, D), F32), jax.ShapeDtypeStruct((T, D), BF16),
                   jax.ShapeDtypeStruct((T, epad), F32), jax.ShapeDtypeStruct(wcast.shape, BF16)],
        compiler_params=_params(("parallel",)),
        name="merge",
    )(ua2, zb2, proj2, proj2, x2, wa, wb, wo, g_ffn.reshape(1, D), wrh, wrl, wcast)


TOKEN_BLOCK = 512


def _route_kernel(lg_ref, tri_ref, lmat_ref, affrow_ref, srow_ref, scol_ref, cnt_ref, *, n_exp, cap):
    S, EP = lg_ref.shape
    nchunk = S // 128
    lane = lax.broadcasted_iota(jnp.int32, (S, EP), 1)
    lg = jnp.where(lane < n_exp, lg_ref[...], -1e30)
    e = jnp.exp(lg - jnp.max(lg, axis=-1, keepdims=True))
    aff = e / jnp.sum(e, axis=-1, keepdims=True)
    rows = [aff[c * 128:(c + 1) * 128, :].T[:n_exp, :] for c in range(nchunk)]
    ar = jnp.concatenate(rows, axis=1)
    affrow_ref[...] = ar

    def count_ge(thr):
        return jnp.sum(jnp.where(ar >= thr, 1.0, 0.0), axis=-1, keepdims=True)

    def cond(state):
        lo, hi, it = state
        mid = 0.5 * (lo + hi)
        open_ = jnp.logical_and(mid != lo, mid != hi)
        return jnp.logical_and(jnp.max(jnp.where(open_, 1.0, 0.0)) > 0.0, it < 400)

    def body(state):
        lo, hi, it = state
        mid = 0.5 * (lo + hi)
        ok = count_ge(mid) >= cap
        return jnp.where(ok, mid, lo), jnp.where(ok, hi, mid), it + 1

    lo0 = jnp.zeros((n_exp, 1), F32)
    hi0 = jnp.full((n_exp, 1), 2.0, F32)
    thr, _, _ = lax.while_loop(cond, body, (lo0, hi0, jnp.int32(0)))

    def cumsum_excl(x):
        xs = jnp.concatenate([x[:, c * 128:(c + 1) * 128] for c in range(nchunk)], axis=0)
        cs = jnp.dot(xs.astype(BF16), tri_ref[...], preferred_element_type=F32)
        off = jnp.dot(lmat_ref[...], cs.astype(BF16), preferred_element_type=F32)[:, 127:128]
        inc = cs + off
        return jnp.concatenate([inc[c * n_exp:(c + 1) * n_exp, :] for c in range(nchunk)], axis=1) - x

    gt = ar > thr
    eq = ar == thr
    need = cap - jnp.sum(jnp.where(gt, 1.0, 0.0), axis=-1, keepdims=True)
    tie_rank = cumsum_excl(jnp.where(eq, 1.0, 0.0))
    sel = jnp.logical_or(gt, jnp.logical_and(eq, tie_rank < need))
    self_ = jnp.where(sel, 1.0, 0.0)
    pos = cumsum_excl(self_)
    slot = jnp.where(sel, pos, -1.0)
    srow_ref[...] = slot.astype(jnp.int32)
    clane = lax.broadcasted_iota(jnp.int32, (n_exp, 128), 1)
    cnt = jnp.where(clane == S // TOKEN_BLOCK, float(cap), 0.0)
    for k in range(S // TOKEN_BLOCK):
        cnt = jnp.where(clane == k, pos[:, k * TOKEN_BLOCK:k * TOKEN_BLOCK + 1], cnt)
    cnt_ref[...] = cnt.astype(jnp.int32)
    slot_p = jnp.concatenate([slot, jnp.full((EP - n_exp, S), -1.0, F32)], axis=0)
    cols = [slot_p[:, c * 128:(c + 1) * 128].T for c in range(nchunk)]
    scol_ref[...] = jnp.concatenate(cols, axis=0).astype(jnp.int32)


def _route(logits3, n_exp, cap):
    B, S, EP = logits3.shape
    nchunk = S // 128
    i = np.arange(128)
    tri = (i[:, None] <= i[None, :]).astype(np.float32)
    r = np.arange(nchunk * n_exp)
    lmat = ((r[:, None] % n_exp == r[None, :] % n_exp) & (r[None, :] // n_exp < r[:, None] // n_exp)).astype(np.float32)
    const = lambda *shape: pl.BlockSpec(shape, lambda b: (0,) * len(shape))
    return pl.pallas_call(
        functools.partial(_route_kernel, n_exp=n_exp, cap=cap),
        grid=(B,),
        in_specs=[pl.BlockSpec((None, S, EP), lambda b: (b, 0, 0)), const(128, 128), const(*lmat.shape)],
        out_specs=[
            pl.BlockSpec((None, n_exp, S), lambda b: (b, 0, 0)),
            pl.BlockSpec((None, n_exp, S), lambda b: (b, 0, 0)),
            pl.BlockSpec((None, S, EP), lambda b: (b, 0, 0)),
            pl.BlockSpec((None, n_exp, 128), lambda b: (b, 0, 0)),
        ],
        out_shape=[
            jax.ShapeDtypeStruct((B, n_exp, S), F32),
            jax.ShapeDtypeStruct((B, n_exp, S), jnp.int32), jax.ShapeDtypeStruct((B, S, EP), jnp.int32),
            jax.ShapeDtypeStruct((B, n_exp, 128), jnp.int32),
        ],
        compiler_params=_params(("parallel",)),
        name="route",
    )(logits3, jnp.asarray(tri, BF16), jnp.asarray(lmat, BF16))


GATHER_ROWS = 128


def _gather_kernel(cnt_ref, h_ref, srow_ref, affrow_ref, wc_ref, xe_ref, val_ref, wco_ref, acc_ref, vacc_ref, *, cap):
    b = pl.program_id(0)
    e = pl.program_id(1)
    S = h_ref.shape[0]
    nblk = S // TOKEN_BLOCK
    base = (b * pl.num_programs(1) + e) * (nblk + 1)
    R = GATHER_ROWS
    wco_ref[...] = wc_ref[...].astype(BF16)

    acc_ref[...] = jnp.zeros_like(acc_ref)
    vacc_ref[...] = jnp.zeros_like(vacc_ref)
    row = lax.broadcasted_iota(jnp.int32, (R, TOKEN_BLOCK), 0)

    def window(k, w0, first):
        tok = slice(k * TOKEN_BLOCK, (k + 1) * TOKEN_BLOCK)
        idx = row + w0
        hit = idx == srow_ref[pl.ds(e, 1), tok]
        if first is not None:
            hit = jnp.logical_and(hit, idx >= first)
        rows = pl.ds(pl.multiple_of(w0, F32_ROWS), R)
        acc_ref[rows, :] += jnp.dot(jnp.where(hit, 1.0, 0.0).astype(BF16), h_ref[tok, :],
                                    preferred_element_type=F32)
        vacc_ref[rows, :] += jnp.sum(jnp.where(hit, affrow_ref[pl.ds(e, 1), tok], 0.0), axis=-1, keepdims=True)

    w0 = [jnp.minimum((cnt_ref[base + k] // F32_ROWS) * F32_ROWS, cap - R) for k in range(nblk)]
    for k in range(nblk):
        window(k, w0[k], None)
    for k in range(nblk):
        def more(w, k=k):
            window(k, jnp.minimum(w, cap - R), w)
            return w + R

        lax.while_loop(lambda w, hi=cnt_ref[base + k + 1]: w < hi, more, w0[k] + R)

    xe_ref[...] = acc_ref[...].astype(xe_ref.dtype)
    val_ref[...] = vacc_ref[...]


def _gather(h3, srow, affrow, cnt, cap, wcast):
    B, S, D = h3.shape
    E = srow.shape[1]
    crow = wcast.shape[0] // (B * E)
    step = lambda b, e, cnt: (b * E + e, 0)
    return pl.pallas_call(
        functools.partial(_gather_kernel, cap=cap),
        grid_spec=pltpu.PrefetchScalarGridSpec(
            num_scalar_prefetch=1,
            grid=(B, E),
            in_specs=[
                pl.BlockSpec((None, S, D), lambda b, e, cnt: (b, 0, 0)),
                pl.BlockSpec((None, E, S), lambda b, e, cnt: (b, 0, 0)),
                pl.BlockSpec((None, E, S), lambda b, e, cnt: (b, 0, 0)),
                pl.BlockSpec((crow, wcast.shape[1]), step),
            ],
            out_specs=[
                pl.BlockSpec((None, None, cap, D), lambda b, e, cnt: (b, e, 0, 0)),
                pl.BlockSpec((None, None, cap, 1), lambda b, e, cnt: (b, e, 0, 0)),
                pl.BlockSpec((crow, wcast.shape[1]), step),
            ],
            scratch_shapes=[pltpu.VMEM((cap, D), F32), pltpu.VMEM((cap, 1), F32)],
        ),
        out_shape=[
            jax.ShapeDtypeStruct((B, E, cap, D), BF16), jax.ShapeDtypeStruct((B, E, cap, 1), F32),
            jax.ShapeDtypeStruct(wcast.shape, BF16),
        ],
        compiler_params=_params(("arbitrary", "arbitrary")),
        name="gather",
    )(cnt, h3, srow, affrow, wcast)


def _ffn_kernel(xe_ref, val_ref, wg_ref, wu_ref, wd_ref, ye_ref):
    xe = xe_ref[...]
    gt = jnp.dot(xe, wg_ref[...], preferred_element_type=F32)
    up = jnp.dot(xe, wu_ref[...], preferred_element_type=F32)
    act = (gt * _sigmoid(gt) * up).astype(BF16)
    ye = jnp.dot(act, wd_ref[...], preferred_element_type=F32)
    ye_ref[...] = (ye * val_ref[...]).astype(ye_ref.dtype)


def _ffn(xe, vals, wg, wu, wd):
    B, E, C, D = xe.shape
    F = wg.shape[2]
    return pl.pallas_call(
        _ffn_kernel,
        grid=(E, B),
        in_specs=[
            pl.BlockSpec((None, None, C, D), lambda e, b: (b, e, 0, 0)),
            pl.BlockSpec((None, None, C, 1), lambda e, b: (b, e, 0, 0)),
            pl.BlockSpec((None, D, F), lambda e, b: (e, 0, 0)),
            pl.BlockSpec((None, D, F), lambda e, b: (e, 0, 0)),
            pl.BlockSpec((None, F, D), lambda e, b: (e, 0, 0)),
        ],
        out_specs=pl.BlockSpec((None, None, C, D), lambda e, b: (b, e, 0, 0)),
        out_shape=jax.ShapeDtypeStruct((B, E, C, D), BF16),
        compiler_params=_params(("parallel", "arbitrary")),
        name="ffn",
    )(xe, vals, wg, wu, wd)


COMBINE_SLOTS = 128


def _combine_kernel(cnt_ref, ye_ref, scol_ref, x1_ref, g_ref, o_ref, extra_ref, *, n_exp, cap):
    b = pl.program_id(0)
    r = pl.program_id(1)
    nblk1 = pl.num_programs(1) + 1
    tm = x1_ref.shape[0]
    KW = COMBINE_SLOTS
    lane = lax.broadcasted_iota(jnp.int32, (tm, KW), 1)
    scol = scol_ref[...]
    extra_ref[...] = jnp.zeros_like(extra_ref)

    def part(e, w0, first):
        idx = lane + w0
        hit = idx == scol[:, e:e + 1]
        if first is not None:
            hit = jnp.logical_and(hit, idx >= first)
        return jnp.where(hit, 1.0, 0.0).astype(BF16), ye_ref[e, pl.ds(pl.multiple_of(w0, BF16_ROWS), KW), :]

    base = [(b * n_exp + e) * nblk1 + r for e in range(n_exp)]
    w0 = [jnp.minimum((cnt_ref[base[e]] // BF16_ROWS) * BF16_ROWS, cap - KW) for e in range(n_exp)]
    acc = x1_ref[...]
    for e in range(0, n_exp, 2):
        (h0, y0), (h1, y1) = part(e, w0[e], None), part(e + 1, w0[e + 1], None)
        acc = acc + jnp.dot(jnp.concatenate([h0, h1], axis=1), jnp.concatenate([y0, y1], axis=0),
                            preferred_element_type=F32)
    for e in range(n_exp):
        def more(w, e=e):
            h, y = part(e, jnp.minimum(w, cap - KW), w)
            extra_ref[...] += jnp.dot(h, y, preferred_element_type=F32)
            return w + KW

        lax.while_loop(lambda w, hi=cnt_ref[base[e] + 1]: w < hi, more, w0[e] + KW)

    acc = acc + extra_ref[...]
    y = acc * lax.rsqrt(jnp.mean(acc * acc, axis=-1, keepdims=True) + EPS)
    o_ref[...] = y * g_ref[...]


def _combine(ye, scol, cnt, x13, g_final):
    B, E, C, D = ye.shape
    S = x13.shape[1]
    EP = scol.shape[2]
    tm = TOKEN_BLOCK
    return pl.pallas_call(
        functools.partial(_combine_kernel, n_exp=E, cap=C),
        grid_spec=pltpu.PrefetchScalarGridSpec(
            num_scalar_prefetch=1,
            grid=(B, S // tm),
            in_specs=[
                pl.BlockSpec((None, E, C, D), lambda b, r, cnt: (b, 0, 0, 0)),
                pl.BlockSpec((None, tm, EP), lambda b, r, cnt: (b, r, 0)),
                pl.BlockSpec((None, tm, D), lambda b, r, cnt: (b, r, 0)),
                pl.BlockSpec((1, D), lambda b, r, cnt: (0, 0)),
            ],
            out_specs=pl.BlockSpec((None, tm, D), lambda b, r, cnt: (b, r, 0)),
            scratch_shapes=[pltpu.VMEM((tm, D), F32)],
        ),
        out_shape=jax.ShapeDtypeStruct((B, S, D), F32),
        compiler_params=_params(("arbitrary", "arbitrary")),
        name="combine",
    )(cnt, ye, scol, x13, g_final.reshape(1, D))


def kernel(x, g_mix, w_in, conv_a_w, conv_a_b, lru_w_r, lru_b_r, lru_w_i, lru_b_i, lru_lambda, w_a_out, conv_b_w, conv_b_b, filt_w1, filt_b1, filt_w2, filt_b2, filt_w3, filt_freq, filt_bias, w_b_out, w_o, g_ffn, w_router, w_gate, w_up, w_down, g_final):
    B, S, D = x.shape
    assert w_in.shape[0] == 1, "single-layer block only"
    l = 0
    lru_w = conv_a_w.shape[2]
    hy_w = filt_bias.shape[2]
    n_exp = w_router.shape[2]
    cap = CAPACITY_FACTOR * S // n_exp
    assert 2 * S == N_FFT
    ma32, ma64, gb, gbi, mai = (jnp.asarray(c).astype(BF16) for c in _dft_constants())

    x2 = x.reshape(B * S, D)
    proj2 = _inproj(x2, g_mix[l], w_in[l])
    proj3 = proj2.reshape(B, S, -1)
    n_ff = w_gate.shape[3]
    ua, wu = _rglru(proj3, conv_a_w[l], conv_a_b[l], lru_w_r[l], lru_b_r[l], lru_w_i[l], lru_b_i[l], lru_lambda[l],
                    w_up[l].reshape(n_exp * D, n_ff))
    kf = _filter_spectra(S, filt_w1[l], filt_b1[l], filt_w2[l], filt_b2[l], filt_w3[l], filt_freq[l], hy_w, ma64, gb)
    zb = _hyena(proj3, 2 * lru_w, conv_b_w[l], conv_b_b[l], filt_bias[l], kf, ma32, gb, gbi, mai)
    x1, h, logits, wd = _merge(ua.reshape(B * S, lru_w), zb.reshape(B * S, hy_w), proj2, 2 * lru_w + 3 * hy_w, x2,
                               w_a_out[l].astype(BF16), w_b_out[l].astype(BF16), w_o[l].astype(BF16), g_ffn[l],
                               w_router[l], w_down[l].reshape(n_exp * n_ff, D))
    affrow, srow, scol, cnt = _route(logits.reshape(B, S, -1), n_exp, cap)
    cnt = cnt[:, :, :S // TOKEN_BLOCK + 1].reshape(-1)
    xe, vals, wg = _gather(h.reshape(B, S, D), srow, affrow, cnt, cap, w_gate[l].reshape(n_exp * D, n_ff))
    ye = _ffn(xe, vals, wg.reshape(n_exp, D, n_ff), wu.reshape(n_exp, D, n_ff), wd.reshape(n_exp, n_ff, D))
    return _combine(ye, scol, cnt, x1.reshape(B, S, D), g_final)
```

```python
import functools
import math

import numpy as np
import jax
import jax.numpy as jnp
from jax import lax
from jax.experimental import pallas as pl
from jax.experimental.pallas import tpu as pltpu

F32 = jnp.float32
BF16 = jnp.bfloat16
EPS = 1e-6
LOG2E = 1.4426950408889634

VMEM_LIMIT_BYTES = 56 * 1024 * 1024
BF16_ROWS = 16
F32_ROWS = 8
LANES = 128

LRU_HEADS = 16
LRU_C = 8.0
CAPACITY_FACTOR = 2
FILTER_BANDS = 16
DECAY_FAST = 0.3
DECAY_SLOW = 1.5
DECAY_TARGET = 1e-2

FFT_N1 = 64
FFT_N2 = 128
N_FFT = FFT_N1 * FFT_N2
FFT_KH = FFT_N1 // 2 + 1
FFT_G = F32_ROWS


def _params(sem):
    return pltpu.CompilerParams(dimension_semantics=sem, vmem_limit_bytes=VMEM_LIMIT_BYTES)


def _inproj_kernel(x_ref, g_ref, w_ref, o_ref, h_ref, wb_ref):
    j = pl.program_id(1)

    @pl.when(pl.program_id(0) == 0)
    def _():
        wb_ref[j] = w_ref[...].astype(BF16)

    @pl.when(j == 0)
    def _():
        x = x_ref[...]
        y = x * lax.rsqrt(jnp.mean(x * x, axis=-1, keepdims=True) + EPS)
        h_ref[...] = (y * g_ref[...]).astype(BF16)

    o_ref[...] = jnp.dot(h_ref[...], wb_ref[j], preferred_element_type=F32).astype(o_ref.dtype)


def _inproj(x2, g, w, tm=1024, tn=1792):
    T, D = x2.shape
    N = w.shape[1]
    nj = N // tn
    return pl.pallas_call(
        _inproj_kernel,
        grid=(T // tm, nj),
        in_specs=[
            pl.BlockSpec((tm, D), lambda i, j: (i, 0)),
            pl.BlockSpec((1, D), lambda i, j: (0, 0)),
            pl.BlockSpec((D, tn), lambda i, j: (0, jnp.where(i == 0, j, nj - 1)), pipeline_mode=pl.Buffered(1)),
        ],
        out_specs=pl.BlockSpec((tm, tn), lambda i, j: (i, j)),
        out_shape=jax.ShapeDtypeStruct((T, N), BF16),
        scratch_shapes=[pltpu.VMEM((tm, D), BF16), pltpu.VMEM((nj, D, tn), BF16)],
        compiler_params=_params(("arbitrary", "arbitrary")),
        name="inproj",
    )(x2, g.reshape(1, D), w)


PAD = BF16_ROWS


def _stage_padded(src_ref, pad_ref):
    S, W = src_ref.shape
    pad_ref[0:PAD, :] = jnp.zeros((PAD, W), pad_ref.dtype)
    pad_ref[PAD + S:PAD + S + PAD, :] = jnp.zeros((PAD, W), pad_ref.dtype)
    pad_ref[PAD:PAD + S, :] = src_ref[...]


def _conv_chunk(pad_ref, r0, rows, taps, w_ref, b_ref):
    n = rows + 2 * PAD
    win = pad_ref[pl.ds(r0, n), :].astype(F32)
    acc = None
    for k, off in enumerate(taps):
        sh = win if off == 0 else pltpu.roll(win, (-off) % n, 0)
        term = sh[PAD:PAD + rows, :] * w_ref[k:k + 1, :]
        acc = term if acc is None else acc + term
    return acc + b_ref[...]


NSEG = 2 * F32_ROWS
SCAN_UNROLL = 8


def _sigmoid(x):
    return 1.0 / (1.0 + jnp.exp2(x * (-LOG2E)))


def _gelu_tanh(x):
    return 0.5 * x * (1.0 + jnp.tanh(math.sqrt(2.0 / math.pi) * (x + 0.044715 * (x * x * x))))


def _softplus(x):
    return jnp.maximum(x, 0.0) + jnp.log(1.0 + jnp.exp(-jnp.abs(x)))


def _ld(ref, rows):
    return jnp.concatenate([ref[l, rows, :] for l in range(ref.shape[0])], axis=1)


def _ld_step(ref, j):
    r = pl.ds(pl.multiple_of(j * F32_ROWS, F32_ROWS), F32_ROWS)
    return jnp.concatenate([jnp.concatenate([ref[l, p, r, :] for l in range(ref.shape[0])], axis=1)
                            for p in range(ref.shape[1])], axis=0)


def _st_step(ref, j, v):
    r = pl.ds(pl.multiple_of(j * F32_ROWS, F32_ROWS), F32_ROWS)
    for p in range(ref.shape[1]):
        for l in range(ref.shape[0]):
            ref[l, p, r, :] = v[p * F32_ROWS:(p + 1) * F32_ROWS, l * LANES:(l + 1) * LANES]


def _st_segment(ref, c, v):
    rows = pl.ds(c % F32_ROWS, v.shape[0], stride=F32_ROWS)
    for l in range(ref.shape[0]):
        ref[l, c // F32_ROWS, rows, :] = v[:, l * LANES:(l + 1) * LANES]


def _st_time(ref, j, pitch, v):
    for p in range(v.shape[0] // F32_ROWS):
        for l in range(ref.shape[0]):
            ref[l, pl.ds(p * F32_ROWS * pitch + j, F32_ROWS, stride=pitch), :] = (
                v[p * F32_ROWS:(p + 1) * F32_ROWS, l * LANES:(l + 1) * LANES])


def _rglru_kernel(ax_ref, gate_ref, cw_ref, cb_ref, wcat_ref, bias_ref, lam_ref, wc_ref, o_ref, wco_ref,
                  pad_ref, af_ref, bf_ref, ab_ref, bb_ref, hs_ref):
    wco_ref[...] = wc_ref[...].astype(BF16)
    S, W = ax_ref.shape
    seg = S // NSEG
    pitch = hs_ref.shape[1] // NSEG
    _stage_padded(ax_ref, pad_ref)
    hc = [(0.5 * LRU_C * LOG2E) * _softplus(-lam_ref[d:d + 1, :]) for d in range(2)]

    def gates(c, _):
        r0 = pl.multiple_of(c * seg, seg)
        xa = _conv_chunk(pad_ref, r0, seg, (-2, -1, 0, 1), cw_ref, cb_ref)
        pre = jnp.dot(xa.astype(BF16), wcat_ref[...], preferred_element_type=F32)
        hx = 0.5 * xa
        pos = r0 + lax.broadcasted_iota(jnp.int32, (seg, W), 0)
        for d, (a_ref, b_ref, start) in enumerate(((af_ref, bf_ref, 0), (ab_ref, bb_ref, S - 1))):
            tr = jnp.tanh(pre[:, (2 * d) * W:(2 * d + 1) * W] + bias_ref[2 * d:2 * d + 1, :])
            ti = jnp.tanh(pre[:, (2 * d + 1) * W:(2 * d + 2) * W] + bias_ref[2 * d + 1:2 * d + 2, :])
            a = jnp.exp2(-hc[d] - hc[d] * tr)
            m2 = 1.0 - a * a
            mult = jnp.where(m2 > 0.0, m2 * lax.rsqrt(m2), 0.0)
            mult = jnp.where(pos == start, 1.0, mult)
            _st_segment(a_ref, c, a)
            _st_segment(b_ref, c, mult * (hx + hx * ti))
        return 0

    lax.fori_loop(0, NSEG, gates, 0)

    def scan(u, carry):
        hf, pf, hb, pb = carry
        for k in range(SCAN_UNROLL):
            jf = u * SCAN_UNROLL + k
            jb = seg - 1 - jf
            a = _ld_step(af_ref, jf)
            hf = a * hf + _ld_step(bf_ref, jf)
            pf = a * pf
            _st_step(bf_ref, jf, hf)
            _st_step(af_ref, jf, pf)
            a = _ld_step(ab_ref, jb)
            hb = a * hb + _ld_step(bb_ref, jb)
            pb = a * pb
            _st_step(bb_ref, jb, hb)
            _st_step(ab_ref, jb, pb)
        return hf, pf, hb, pb

    zero = jnp.zeros((NSEG, W), F32)
    one = jnp.ones((NSEG, W), F32)
    hf, pf, hb, pb = lax.fori_loop(0, seg // SCAN_UNROLL, scan, (zero, one, zero, one))

    row = lax.broadcasted_iota(jnp.int32, (NSEG, W), 0)
    cf = zero
    cb = zero
    for _ in range(NSEG - 1):
        cf = jnp.where(row >= 1, pltpu.roll(hf + pf * cf, 1, 0), 0.0)
        cb = jnp.where(row < NSEG - 1, pltpu.roll(hb + pb * cb, NSEG - 1, 0), 0.0)

    def fix(u, _):
        for k in range(SCAN_UNROLL):
            j = u * SCAN_UNROLL + k
            h = (_ld_step(bf_ref, j) + _ld_step(af_ref, j) * cf + _ld_step(bb_ref, j) + _ld_step(ab_ref, j) * cb)
            _st_time(hs_ref, j, pitch, h)
        return 0

    lax.fori_loop(0, seg // SCAN_UNROLL, fix, 0)

    def out(c, _):
        r0 = pl.multiple_of(c * seg, seg)
        h = _ld(hs_ref, pl.ds(pl.multiple_of(c * pitch, F32_ROWS), seg))
        g = gate_ref[pl.ds(r0, seg), :].astype(F32)
        o_ref[pl.ds(r0, seg), :] = (h * _gelu_tanh(g)).astype(o_ref.dtype)
        return 0

    lax.fori_loop(0, NSEG, out, 0)


def _rglru(proj3, conv_w, conv_b, w_r, b_r, w_i, b_i, lam, wcast, wt=256):
    B, S, _ = proj3.shape
    W = conv_w.shape[1]
    nj = W // wt
    crow = wcast.shape[0] // (B * nj)
    cstep = lambda b, j: (b * nj + j, 0)
    hd = W // LRU_HEADS
    hpt = wt // hd
    pitch = S // NSEG + F32_ROWS

    def blockdiag(w):
        w4 = w.reshape(nj, hpt, hd, hd)
        eye = jnp.eye(hpt, dtype=w.dtype)
        return jnp.einsum('jhab,hg->jhagb', w4, eye).reshape(nj, wt, wt)

    wcat = (0.5 * jnp.concatenate([blockdiag(w_r[0]), blockdiag(w_i[0]), blockdiag(w_r[1]), blockdiag(w_i[1])],
                                  axis=-1)).astype(BF16)
    bias = 0.5 * jnp.stack([b_r[0], b_i[0], b_r[1], b_i[1]], axis=0)
    gate_off = W // wt
    return pl.pallas_call(
        _rglru_kernel,
        grid=(B, nj),
        in_specs=[
            pl.BlockSpec((None, S, wt), lambda b, j: (b, 0, j)),
            pl.BlockSpec((None, S, wt), lambda b, j: (b, 0, gate_off + j)),
            pl.BlockSpec((conv_w.shape[0], wt), lambda b, j: (0, j)),
            pl.BlockSpec((1, wt), lambda b, j: (0, j)),
            pl.BlockSpec((None, wt, 4 * wt), lambda b, j: (j, 0, 0)),
            pl.BlockSpec((4, wt), lambda b, j: (0, j)),
            pl.BlockSpec((2, wt), lambda b, j: (0, j)),
            pl.BlockSpec((crow, wcast.shape[1]), cstep),
        ],
        out_specs=[pl.BlockSpec((None, S, wt), lambda b, j: (b, 0, j)), pl.BlockSpec((crow, wcast.shape[1]), cstep)],
        out_shape=[jax.ShapeDtypeStruct((B, S, W), BF16), jax.ShapeDtypeStruct(wcast.shape, BF16)],
        scratch_shapes=[pltpu.VMEM((S + 2 * PAD, wt), BF16)] + [pltpu.VMEM((wt // LANES, NSEG // F32_ROWS, S // (NSEG // F32_ROWS), LANES), F32)] * 4
        + [pltpu.VMEM((wt // LANES, NSEG * pitch, LANES), F32)],
        compiler_params=_params(("parallel", "parallel")),
        name="rglru",
    )(proj3, proj3, conv_w, conv_b.reshape(1, W), wcat, bias, lam, wcast)


COMPS_FWD = [(c, k1) for c in range(2) for k1 in range(FFT_KH)]
COMPS_INV = [(c, k1) for (c, k1) in COMPS_FWD if not (c == 1 and k1 in (0, FFT_KH - 1))]


@functools.lru_cache(maxsize=None)
def _dft_constants():
    g = FFT_G
    eye = np.eye(g)

    def stage_a(n1_in):
        k1 = np.arange(FFT_KH)[:, None]
        n1 = np.arange(n1_in)[None, :]
        ang = -2.0 * np.pi * ((k1 * n1) % FFT_N1) / FFT_N1
        a = np.stack([np.cos(ang), np.sin(ang)])
        a = np.stack([a[c, k] for (c, k) in COMPS_FWD])
        return np.einsum('qn,st->qsnt', a, eye).reshape(len(COMPS_FWD) * g, n1_in * g)

    k1 = np.arange(FFT_KH)[:, None, None]
    k2 = np.arange(FFT_N2)[None, :, None]
    n2 = np.arange(FFT_N2)[None, None, :]
    ang = -2.0 * np.pi * ((n2 * (k1 + FFT_N1 * k2)) % N_FFT) / N_FFT
    gr, gi = np.cos(ang), np.sin(ang)
    gb = np.concatenate([np.concatenate([gr, -gi], axis=2), np.concatenate([gi, gr], axis=2)], axis=1)
    gbi = np.transpose(gb, (0, 2, 1))

    c = np.full(FFT_KH, 2.0)
    c[0] = 1.0
    c[-1] = 1.0
    n1 = np.arange(FFT_N1 // 2)[:, None]
    kk = np.arange(FFT_KH)[None, :]
    ang = 2.0 * np.pi * ((n1 * kk) % FFT_N1) / FFT_N1
    bi = np.stack([np.cos(ang) * c / N_FFT, -np.sin(ang) * c / N_FFT])
    bi = np.stack([bi[c_, :, k] for (c_, k) in COMPS_INV], axis=1)
    mai = np.einsum('nq,st->nsqt', bi, eye).reshape((FFT_N1 // 2) * g, len(COMPS_INV) * g)
    f = np.float32
    return stage_a(FFT_N1 // 2).astype(f), stage_a(FFT_N1).astype(f), gb.astype(f), gbi.astype(f), mai.astype(f)


def _split_rows(tiles):
    f = [t.astype(F32) for t in tiles]
    return [jnp.concatenate([t[h * FFT_G:(h + 1) * FFT_G] for t in f], axis=0).astype(BF16)
            for h in range(BF16_ROWS // FFT_G)]


def _join_rows(outs, q):
    return jnp.concatenate([o[q * FFT_G:(q + 1) * FFT_G] for o in outs], axis=0)


def _stage_a_fwd(src_ref, ma_ref, y_ref, n1_in):
    for p in range(FFT_N2 // BF16_ROWS):
        r = BF16_ROWS * p
        opnds = _split_rows([src_ref[FFT_N2 * n1 + r:FFT_N2 * n1 + r + BF16_ROWS, :] for n1 in range(n1_in)])
        outs = [jnp.dot(ma_ref[...], o, preferred_element_type=F32) for o in opnds]
        for q, (c, k1) in enumerate(COMPS_FWD):
            y_ref[k1, c * FFT_N2 + r:c * FFT_N2 + r + BF16_ROWS, :] = _join_rows(outs, q).astype(y_ref.dtype)


def _stage_a_inv(y_ref, mai_ref, p):
    r = BF16_ROWS * p
    opnds = _split_rows([y_ref[k1, c * FFT_N2 + r:c * FFT_N2 + r + BF16_ROWS, :] for (c, k1) in COMPS_INV])
    outs = [jnp.dot(mai_ref[...], o, preferred_element_type=F32) for o in opnds]
    return [_join_rows(outs, n1) for n1 in range(FFT_N1 // 2)]


def _stage_b_fwd_slab(gb_ref, y_ref, k1):
    return jnp.dot(gb_ref[k1], y_ref[k1], preferred_element_type=F32)


def _dot3(a, b):
    ah = a.astype(BF16)
    al = (a - ah.astype(F32)).astype(BF16)
    bh = b.astype(BF16)
    bl = (b - bh.astype(F32)).astype(BF16)
    d = lambda u, v: jnp.dot(u, v, preferred_element_type=F32)
    return d(ah, bh) + d(al, bh) + d(ah, bl)


def _filtmlp_kernel(fa_ref, fb_ref, w1_ref, b1_ref, w2_ref, b2_ref, freq_ref, o_ref):
    u = jnp.concatenate([_dot3(fa_ref[...], w1_ref[...]), _dot3(fb_ref[...], w1_ref[...])], axis=1)
    h = jnp.sin(freq_ref[0:1, :] * (u + b1_ref[...]))
    o_ref[...] = jnp.sin(freq_ref[1:2, :] * (_dot3(h, w2_ref[...]) + b2_ref[...]))


def _filtfft_kernel(h_ref, t_ref, w3f_ref, w3b_ref, delta_ref, ma_ref, gb_ref, o_ref, kc_ref, y_ref, *, chunk):
    n, W = kc_ref.shape
    L = n // 2

    def body(c, _):
        r0 = pl.multiple_of(c * chunk, chunk)
        h = h_ref[pl.ds(r0, chunk), :].astype(BF16)
        kf = jnp.dot(h, w3f_ref[...], preferred_element_type=F32)
        kb = jnp.dot(h, w3b_ref[...], preferred_element_type=F32)
        row = r0 + lax.broadcasted_iota(jnp.int32, (chunk, W), 0)
        kb = jnp.where(row == 0, 0.0, kb)
        kc_ref[pl.ds(r0, chunk), :] = (kf * jnp.exp(-t_ref[pl.ds(r0, chunk), :] * delta_ref[...])).astype(kc_ref.dtype)
        kc_ref[pl.ds(L + r0, chunk), :] = (kb * jnp.exp(-t_ref[pl.ds(L + r0, chunk), :] * delta_ref[...])).astype(kc_ref.dtype)
        return 0

    lax.fori_loop(0, L // chunk, body, 0)
    _stage_a_fwd(kc_ref, ma_ref, y_ref, FFT_N1)

    def slab(k1, _):
        o_ref[k1] = _stage_b_fwd_slab(gb_ref, y_ref, k1).astype(o_ref.dtype)
        return 0

    lax.fori_loop(0, FFT_KH, slab, 0, unroll=11)


def _filter_spectra(L, w1, b1, w2, b2, w3, freq, W, ma64, gb, wt=256, chunk=512):
    f32 = F32
    order = w3.shape[1] // (2 * W)
    hid = w1.shape[1]
    t = jnp.linspace(0.0, 1.0, L, dtype=f32)[:, None]
    w = (2.0 * math.pi / L) * jnp.arange(L, dtype=f32)[:, None]
    f = jnp.linspace(1e-4, FILTER_BANDS - 1, FILTER_BANDS, dtype=f32)[None, :]
    z = jnp.concatenate([t, jnp.cos(w * f), -jnp.sin(w * f)], axis=-1)
    emb = z.shape[1]
    embp = 128
    z = jnp.pad(z, ((0, 0), (0, embp - emb)))
    feat = jnp.concatenate([z, z[:1], z[:0:-1]], axis=0)
    tt = jnp.concatenate([t, t[:1], t[:0:-1]], axis=0)
    w1p = jnp.pad(w1, ((0, embp - emb), (0, 0)))
    zero = jnp.zeros_like(w2)
    w2d = jnp.concatenate([jnp.concatenate([w2, zero], axis=1), jnp.concatenate([zero, w2], axis=1)], axis=0)
    two = lambda v: jnp.concatenate([v, v], axis=-1)
    nblk = L // chunk
    const1 = lambda *shape: pl.BlockSpec(shape, lambda i: (0,) * len(shape))
    hfeat = pl.pallas_call(
        _filtmlp_kernel,
        grid=(nblk,),
        in_specs=[
            pl.BlockSpec((chunk, embp), lambda i: (i, 0)), pl.BlockSpec((chunk, embp), lambda i: (nblk + i, 0)),
            const1(embp, hid), const1(1, 2 * hid), const1(2 * hid, 2 * hid), const1(1, 2 * hid), const1(2, 2 * hid),
        ],
        out_specs=pl.BlockSpec((chunk, 2 * hid), lambda i: (i, 0)),
        out_shape=jax.ShapeDtypeStruct((L, 2 * hid), f32),
        compiler_params=_params(("parallel",)),
        name="filtmlp",
    )(feat, feat, w1p, two(b1.reshape(1, hid)), w2d, two(b2.reshape(1, hid)), two(freq))

    w3b16 = w3.astype(BF16)
    zero3 = jnp.zeros_like(w3b16)
    w3f = jnp.concatenate([w3b16, zero3], axis=0)
    w3b = jnp.concatenate([zero3, w3b16], axis=0)
    min_decay = math.log(DECAY_TARGET) / DECAY_SLOW
    max_decay = math.log(DECAY_TARGET) / DECAY_FAST
    deltas = jnp.abs(jnp.linspace(min_decay, max_decay, W, dtype=f32))[None, :]
    nj = W // wt
    n = 2 * L
    const = lambda *shape: pl.BlockSpec(shape, lambda o, j: (0,) * len(shape))
    return pl.pallas_call(
        functools.partial(_filtfft_kernel, chunk=chunk),
        grid=(order, nj),
        in_specs=[
            const(L, 2 * hid), const(n, 1),
            pl.BlockSpec((2 * hid, wt), lambda o, j: (0, (2 * o) * nj + j)),
            pl.BlockSpec((2 * hid, wt), lambda o, j: (0, (2 * o + 1) * nj + j)),
            pl.BlockSpec((1, wt), lambda o, j: (0, j)),
            const(*ma64.shape), const(*gb.shape),
        ],
        out_specs=pl.BlockSpec((None, None, FFT_KH, 2 * FFT_N2, wt), lambda o, j: (o, j, 0, 0, 0)),
        out_shape=jax.ShapeDtypeStruct((order, nj, FFT_KH, 2 * FFT_N2, wt), BF16),
        scratch_shapes=[pltpu.VMEM((n, wt), BF16), pltpu.VMEM((FFT_KH, 2 * FFT_N2, wt), BF16)],
        compiler_params=_params(("parallel", "parallel")),
        name="filtfft",
    )(hfeat, tt, w3f, w3b, deltas, ma64, gb)


def _hyena_kernel(v_ref, x1_ref, x2_ref, cw_ref, cb_ref, fb_ref, kf_ref, ma_ref, gb_ref, gbi_ref, mai_ref, o_ref,
                  pad_ref, z_ref, g1_ref, g2_ref, y_ref, *, chunk):
    S, W = v_ref.shape
    g = BF16_ROWS

    for idx, (src, dst) in enumerate(((v_ref, z_ref), (x1_ref, g1_ref), (x2_ref, g2_ref))):
        _stage_padded(src, pad_ref)

        def conv(c, _, idx=idx, dst=dst):
            r0 = pl.multiple_of(c * chunk, chunk)
            y = _conv_chunk(pad_ref, r0, chunk, (-1, 0, 1), cw_ref.at[idx], cb_ref.at[idx])
            dst[pl.ds(r0, chunk), :] = y.astype(dst.dtype)
            return 0

        lax.fori_loop(0, S // chunk, conv, 0)

    for order, (gate_ref, dst) in enumerate(((g1_ref, z_ref), (g2_ref, o_ref))):
        _stage_a_fwd(z_ref, ma_ref, y_ref, FFT_N1 // 2)

        def slab(k1, _, order=order):
            x = _stage_b_fwd_slab(gb_ref, y_ref, k1)
            kf = kf_ref[order, k1].astype(F32)
            xr, xi = x[:FFT_N2], x[FFT_N2:]
            kr, ki = kf[:FFT_N2], kf[FFT_N2:]
            p = jnp.concatenate([xr * kr - xi * ki, xr * ki + xi * kr], axis=0).astype(BF16)
            y_ref[k1] = jnp.dot(gbi_ref[k1], p, preferred_element_type=F32).astype(y_ref.dtype)
            return 0

        lax.fori_loop(0, FFT_KH, slab, 0, unroll=True)

        bias = fb_ref[order:order + 1, :]
        for p in range(FFT_N2 // g):
            for n1, y in enumerate(_stage_a_inv(y_ref, mai_ref, p)):
                t0 = FFT_N2 * n1 + g * p
                zt = z_ref[t0:t0 + g, :].astype(F32)
                gt = gate_ref[t0:t0 + g, :].astype(F32)
                dst[t0:t0 + g, :] = (gt * (y + bias * zt)).astype(dst.dtype)


def _hyena(proj3, col0, conv_w, conv_b, filt_bias, kf, ma32, gb, gbi, mai, wt=256, chunk=512):
    B, S, _ = proj3.shape
    W = filt_bias.shape[1]
    nj = W // wt
    c0 = col0 // wt
    cw = conv_w.reshape(conv_w.shape[0], 3, W).transpose(1, 0, 2)
    cb = conv_b.reshape(3, 1, W)
    const = lambda *shape: pl.BlockSpec(shape, lambda j, b: (0,) * len(shape), pipeline_mode=pl.Buffered(1))
    return pl.pallas_call(
        functools.partial(_hyena_kernel, chunk=chunk),
        grid=(nj, B),
        in_specs=[
            pl.BlockSpec((None, S, wt), lambda j, b: (b, 0, c0 + j)),
            pl.BlockSpec((None, S, wt), lambda j, b: (b, 0, c0 + nj + j)),
            pl.BlockSpec((None, S, wt), lambda j, b: (b, 0, c0 + 2 * nj + j)),
            pl.BlockSpec((3, conv_w.shape[0], wt), lambda j, b: (0, 0, j)),
            pl.BlockSpec((3, 1, wt), lambda j, b: (0, 0, j)),
            pl.BlockSpec((filt_bias.shape[0], wt), lambda j, b: (0, j)),
            pl.BlockSpec((kf.shape[0], None, FFT_KH, 2 * FFT_N2, wt), lambda j, b: (0, j, 0, 0, 0),
                         pipeline_mode=pl.Buffered(1)),
            const(*ma32.shape), const(*gb.shape), const(*gbi.shape), const(*mai.shape),
        ],
        out_specs=pl.BlockSpec((None, S, wt), lambda j, b: (b, 0, j)),
        out_shape=jax.ShapeDtypeStruct((B, S, W), BF16),
        scratch_shapes=[pltpu.VMEM((S + 2 * PAD, wt), BF16)] + [pltpu.VMEM((S, wt), BF16)] * 3
        + [pltpu.VMEM((FFT_KH, 2 * FFT_N2, wt), BF16)],
        compiler_params=_params(("parallel", "arbitrary")),
        name="hyena",
    )(proj3, proj3, proj3, cw, cb, filt_bias, kf, ma32, gb, gbi, mai)


MERGE_SUB = 2


def _merge_kernel(ua_ref, zb_ref, ga_ref, gbr_ref, x_ref, wa_ref, wb_ref, wo_ref, g_ref, wrh_ref, wrl_ref, wc_ref,
                  x1_ref, h_ref, lg_ref, wco_ref):
    wco_ref[...] = wc_ref[...].astype(BF16)
    sub = x_ref.shape[0] // MERGE_SUB
    for s in range(MERGE_SUB):
        rows = slice(s * sub, (s + 1) * sub)
        ya = jnp.dot(ua_ref[rows, :], wa_ref[...], preferred_element_type=F32)
        yb = jnp.dot(zb_ref[rows, :], wb_ref[...], preferred_element_type=F32)
        m = _sigmoid(ga_ref[rows, :].astype(F32)) * ya + _sigmoid(gbr_ref[rows, :].astype(F32)) * yb
        x1 = x_ref[rows, :] + jnp.dot(m.astype(BF16), wo_ref[...], preferred_element_type=F32)
        x1_ref[rows, :] = x1
        h = x1 * lax.rsqrt(jnp.mean(x1 * x1, axis=-1, keepdims=True) + EPS) * g_ref[...]
        hh = h.astype(BF16)
        h_ref[rows, :] = hh
        hl = (h - hh.astype(F32)).astype(BF16)
        lg_ref[rows, :] = (jnp.dot(hh, wrh_ref[...], preferred_element_type=F32)
                           + jnp.dot(hl, wrh_ref[...], preferred_element_type=F32)
                           + jnp.dot(hh, wrl_ref[...], preferred_element_type=F32))


def _merge(ua2, zb2, proj2, gate_col0, x2, wa, wb, wo, g_ffn, w_router, wcast, tm=512, epad=128):
    T, D = x2.shape
    crow = wcast.shape[0] // (T // tm)
    gc = gate_col0 // D
    E = w_router.shape[1]
    wr = jnp.pad(w_router, ((0, 0), (0, epad - E)))
    wrh = wr.astype(BF16)
    wrl = (wr - wrh.astype(F32)).astype(BF16)
    row = lambda i: (i, 0)
    const = lambda *shape: pl.BlockSpec(shape, lambda i: (0,) * len(shape))
    return pl.pallas_call(
        _merge_kernel,
        grid=(T // tm,),
        in_specs=[
            pl.BlockSpec((tm, D), row), pl.BlockSpec((tm, D), row),
            pl.BlockSpec((tm, D), lambda i: (i, gc)), pl.BlockSpec((tm, D), lambda i: (i, gc + 1)),
            pl.BlockSpec((tm, D), row),
            const(D, D), const(D, D), const(D, D), const(1, D), const(D, epad), const(D, epad),
            pl.BlockSpec((crow, wcast.shape[1]), row),
        ],
        out_specs=[pl.BlockSpec((tm, D), row), pl.BlockSpec((tm, D), row), pl.BlockSpec((tm, epad), row),
                   pl.BlockSpec((crow, wcast.shape[1]), row)],
        out_shape=[jax.ShapeDtypeStruct((T, D), F32), jax.ShapeDtypeStruct((T, D), BF16),
                   jax.ShapeDtypeStruct((T, epad), F32), jax.ShapeDtypeStruct(wcast.shape, BF16)],
        compiler_params=_params(("parallel",)),
        name="merge",
    )(ua2, zb2, proj2, proj2, x2, wa, wb, wo, g_ffn.reshape(1, D), wrh, wrl, wcast)


TOKEN_BLOCK = 512


def _route_kernel(lg_ref, tri_ref, lmat_ref, affrow_ref, srow_ref, scol_ref, cnt_ref, *, n_exp, cap):
    S, EP = lg_ref.shape
    nchunk = S // 128
    lane = lax.broadcasted_iota(jnp.int32, (S, EP), 1)
    lg = jnp.where(lane < n_exp, lg_ref[...], -1e30)
    e = jnp.exp(lg - jnp.max(lg, axis=-1, keepdims=True))
    aff = e / jnp.sum(e, axis=-1, keepdims=True)
    rows = [aff[c * 128:(c + 1) * 128, :].T[:n_exp, :] for c in range(nchunk)]
    ar = jnp.concatenate(rows, axis=1)
    affrow_ref[...] = ar

    def count_ge(thr):
        return jnp.sum(jnp.where(ar >= thr, 1.0, 0.0), axis=-1, keepdims=True)

    def cond(state):
        lo, hi, it = state
        mid = 0.5 * (lo + hi)
        open_ = jnp.logical_and(mid != lo, mid != hi)
        return jnp.logical_and(jnp.max(jnp.where(open_, 1.0, 0.0)) > 0.0, it < 400)

    def body(state):
        lo, hi, it = state
        mid = 0.5 * (lo + hi)
        ok = count_ge(mid) >= cap
        return jnp.where(ok, mid, lo), jnp.where(ok, hi, mid), it + 1

    lo0 = jnp.zeros((n_exp, 1), F32)
    hi0 = jnp.full((n_exp, 1), 2.0, F32)
    thr, _, _ = lax.while_loop(cond, body, (lo0, hi0, jnp.int32(0)))

    def cumsum_excl(x):
        xs = jnp.concatenate([x[:, c * 128:(c + 1) * 128] for c in range(nchunk)], axis=0)
        cs = jnp.dot(xs.astype(BF16), tri_ref[...], preferred_element_type=F32)
        off = jnp.dot(lmat_ref[...], cs.astype(BF16), preferred_element_type=F32)[:, 127:128]
        inc = cs + off
        return jnp.concatenate([inc[c * n_exp:(c + 1) * n_exp, :] for c in range(nchunk)], axis=1) - x

    gt = ar > thr
    eq = ar == thr
    need = cap - jnp.sum(jnp.where(gt, 1.0, 0.0), axis=-1, keepdims=True)
    tie_rank = cumsum_excl(jnp.where(eq, 1.0, 0.0))
    sel = jnp.logical_or(gt, jnp.logical_and(eq, tie_rank < need))
    self_ = jnp.where(sel, 1.0, 0.0)
    pos = cumsum_excl(self_)
    slot = jnp.where(sel, pos, -1.0)
    srow_ref[...] = slot.astype(jnp.int32)
    clane = lax.broadcasted_iota(jnp.int32, (n_exp, 128), 1)
    cnt = jnp.where(clane == S // TOKEN_BLOCK, float(cap), 0.0)
    for k in range(S // TOKEN_BLOCK):
        cnt = jnp.where(clane == k, pos[:, k * TOKEN_BLOCK:k * TOKEN_BLOCK + 1], cnt)
    cnt_ref[...] = cnt.astype(jnp.int32)
    slot_p = jnp.concatenate([slot, jnp.full((EP - n_exp, S), -1.0, F32)], axis=0)
    cols = [slot_p[:, c * 128:(c + 1) * 128].T for c in range(nchunk)]
    scol_ref[...] = jnp.concatenate(cols, axis=0).astype(jnp.int32)


def _route(logits3, n_exp, cap):
    B, S, EP = logits3.shape
    nchunk = S // 128
    i = np.arange(128)
    tri = (i[:, None] <= i[None, :]).astype(np.float32)
    r = np.arange(nchunk * n_exp)
    lmat = ((r[:, None] % n_exp == r[None, :] % n_exp) & (r[None, :] // n_exp < r[:, None] // n_exp)).astype(np.float32)
    const = lambda *shape: pl.BlockSpec(shape, lambda b: (0,) * len(shape))
    return pl.pallas_call(
        functools.partial(_route_kernel, n_exp=n_exp, cap=cap),
        grid=(B,),
        in_specs=[pl.BlockSpec((None, S, EP), lambda b: (b, 0, 0)), const(128, 128), const(*lmat.shape)],
        out_specs=[
            pl.BlockSpec((None, n_exp, S), lambda b: (b, 0, 0)),
            pl.BlockSpec((None, n_exp, S), lambda b: (b, 0, 0)),
            pl.BlockSpec((None, S, EP), lambda b: (b, 0, 0)),
            pl.BlockSpec((None, n_exp, 128), lambda b: (b, 0, 0)),
        ],
        out_shape=[
            jax.ShapeDtypeStruct((B, n_exp, S), F32),
            jax.ShapeDtypeStruct((B, n_exp, S), jnp.int32), jax.ShapeDtypeStruct((B, S, EP), jnp.int32),
            jax.ShapeDtypeStruct((B, n_exp, 128), jnp.int32),
        ],
        compiler_params=_params(("parallel",)),
        name="route",
    )(logits3, jnp.asarray(tri, BF16), jnp.asarray(lmat, BF16))


GATHER_ROWS = 128


def _gather_kernel(cnt_ref, h_ref, srow_ref, affrow_ref, wc_ref, xe_ref, val_ref, wco_ref, acc_ref, vacc_ref, *, cap):
    b = pl.program_id(0)
    e = pl.program_id(1)
    S = h_ref.shape[0]
    nblk = S // TOKEN_BLOCK
    base = (b * pl.num_programs(1) + e) * (nblk + 1)
    R = GATHER_ROWS
    wco_ref[...] = wc_ref[...].astype(BF16)

    acc_ref[...] = jnp.zeros_like(acc_ref)
    vacc_ref[...] = jnp.zeros_like(vacc_ref)
    row = lax.broadcasted_iota(jnp.int32, (R, TOKEN_BLOCK), 0)

    def window(k, w0, first):
        tok = slice(k * TOKEN_BLOCK, (k + 1) * TOKEN_BLOCK)
        idx = row + w0
        hit = idx == srow_ref[pl.ds(e, 1), tok]
        if first is not None:
            hit = jnp.logical_and(hit, idx >= first)
        rows = pl.ds(pl.multiple_of(w0, F32_ROWS), R)
        acc_ref[rows, :] += jnp.dot(jnp.where(hit, 1.0, 0.0).astype(BF16), h_ref[tok, :],
                                    preferred_element_type=F32)
        vacc_ref[rows, :] += jnp.sum(jnp.where(hit, affrow_ref[pl.ds(e, 1), tok], 0.0), axis=-1, keepdims=True)

    w0 = [jnp.minimum((cnt_ref[base + k] // F32_ROWS) * F32_ROWS, cap - R) for k in range(nblk)]
    for k in range(nblk):
        window(k, w0[k], None)
    for k in range(nblk):
        def more(w, k=k):
            window(k, jnp.minimum(w, cap - R), w)
            return w + R

        lax.while_loop(lambda w, hi=cnt_ref[base + k + 1]: w < hi, more, w0[k] + R)

    xe_ref[...] = acc_ref[...].astype(xe_ref.dtype)
    val_ref[...] = vacc_ref[...]


def _gather(h3, srow, affrow, cnt, cap, wcast):
    B, S, D = h3.shape
    E = srow.shape[1]
    crow = wcast.shape[0] // (B * E)
    step = lambda b, e, cnt: (b * E + e, 0)
    return pl.pallas_call(
        functools.partial(_gather_kernel, cap=cap),
        grid_spec=pltpu.PrefetchScalarGridSpec(
            num_scalar_prefetch=1,
            grid=(B, E),
            in_specs=[
                pl.BlockSpec((None, S, D), lambda b, e, cnt: (b, 0, 0)),
                pl.BlockSpec((None, E, S), lambda b, e, cnt: (b, 0, 0)),
                pl.BlockSpec((None, E, S), lambda b, e, cnt: (b, 0, 0)),
                pl.BlockSpec((crow, wcast.shape[1]), step),
            ],
            out_specs=[
                pl.BlockSpec((None, None, cap, D), lambda b, e, cnt: (b, e, 0, 0)),
                pl.BlockSpec((None, None, cap, 1), lambda b, e, cnt: (b, e, 0, 0)),
                pl.BlockSpec((crow, wcast.shape[1]), step),
            ],
            scratch_shapes=[pltpu.VMEM((cap, D), F32), pltpu.VMEM((cap, 1), F32)],
        ),
        out_shape=[
            jax.ShapeDtypeStruct((B, E, cap, D), BF16), jax.ShapeDtypeStruct((B, E, cap, 1), F32),
            jax.ShapeDtypeStruct(wcast.shape, BF16),
        ],
        compiler_params=_params(("arbitrary", "arbitrary")),
        name="gather",
    )(cnt, h3, srow, affrow, wcast)


def _ffn_kernel(xe_ref, val_ref, wg_ref, wu_ref, wd_ref, ye_ref):
    xe = xe_ref[...]
    gt = jnp.dot(xe, wg_ref[...], preferred_element_type=F32)
    up = jnp.dot(xe, wu_ref[...], preferred_element_type=F32)
    act = (gt * _sigmoid(gt) * up).astype(BF16)
    ye = jnp.dot(act, wd_ref[...], preferred_element_type=F32)
    ye_ref[...] = (ye * val_ref[...]).astype(ye_ref.dtype)


def _ffn(xe, vals, wg, wu, wd):
    B, E, C, D = xe.shape
    F = wg.shape[2]
    return pl.pallas_call(
        _ffn_kernel,
        grid=(E, B),
        in_specs=[
            pl.BlockSpec((None, None, C, D), lambda e, b: (b, e, 0, 0)),
            pl.BlockSpec((None, None, C, 1), lambda e, b: (b, e, 0, 0)),
            pl.BlockSpec((None, D, F), lambda e, b: (e, 0, 0)),
            pl.BlockSpec((None, D, F), lambda e, b: (e, 0, 0)),
            pl.BlockSpec((None, F, D), lambda e, b: (e, 0, 0)),
        ],
        out_specs=pl.BlockSpec((None, None, C, D), lambda e, b: (b, e, 0, 0)),
        out_shape=jax.ShapeDtypeStruct((B, E, C, D), BF16),
        compiler_params=_params(("parallel", "arbitrary")),
        name="ffn",
    )(xe, vals, wg, wu, wd)


COMBINE_SLOTS = 128


def _combine_kernel(cnt_ref, ye_ref, scol_ref, x1_ref, g_ref, o_ref, extra_ref, *, n_exp, cap):
    b = pl.program_id(0)
    r = pl.program_id(1)
    nblk1 = pl.num_programs(1) + 1
    tm = x1_ref.shape[0]
    KW = COMBINE_SLOTS
    lane = lax.broadcasted_iota(jnp.int32, (tm, KW), 1)
    scol = scol_ref[...]
    extra_ref[...] = jnp.zeros_like(extra_ref)

    def part(e, w0, first):
        idx = lane + w0
        hit = idx == scol[:, e:e + 1]
        if first is not None:
            hit = jnp.logical_and(hit, idx >= first)
        return jnp.where(hit, 1.0, 0.0).astype(BF16), ye_ref[e, pl.ds(pl.multiple_of(w0, BF16_ROWS), KW), :]

    base = [(b * n_exp + e) * nblk1 + r for e in range(n_exp)]
    w0 = [jnp.minimum((cnt_ref[base[e]] // BF16_ROWS) * BF16_ROWS, cap - KW) for e in range(n_exp)]
    acc = x1_ref[...]
    for e in range(0, n_exp, 2):
        (h0, y0), (h1, y1) = part(e, w0[e], None), part(e + 1, w0[e + 1], None)
        acc = acc + jnp.dot(jnp.concatenate([h0, h1], axis=1), jnp.concatenate([y0, y1], axis=0),
                            preferred_element_type=F32)
    for e in range(n_exp):
        def more(w, e=e):
            h, y = part(e, jnp.minimum(w, cap - KW), w)
            extra_ref[...] += jnp.dot(h, y, preferred_element_type=F32)
            return w + KW

        lax.while_loop(lambda w, hi=cnt_ref[base[e] + 1]: w < hi, more, w0[e] + KW)

    acc = acc + extra_ref[...]
    y = acc * lax.rsqrt(jnp.mean(acc * acc, axis=-1, keepdims=True) + EPS)
    o_ref[...] = y * g_ref[...]


def _combine(ye, scol, cnt, x13, g_final):
    B, E, C, D = ye.shape
    S = x13.shape[1]
    EP = scol.shape[2]
    tm = TOKEN_BLOCK
    return pl.pallas_call(
        functools.partial(_combine_kernel, n_exp=E, cap=C),
        grid_spec=pltpu.PrefetchScalarGridSpec(
            num_scalar_prefetch=1,
            grid=(B, S // tm),
            in_specs=[
                pl.BlockSpec((None, E, C, D), lambda b, r, cnt: (b, 0, 0, 0)),
                pl.BlockSpec((None, tm, EP), lambda b, r, cnt: (b, r, 0)),
                pl.BlockSpec((None, tm, D), lambda b, r, cnt: (b, r, 0)),
                pl.BlockSpec((1, D), lambda b, r, cnt: (0, 0)),
            ],
            out_specs=pl.BlockSpec((None, tm, D), lambda b, r, cnt: (b, r, 0)),
            scratch_shapes=[pltpu.VMEM((tm, D), F32)],
        ),
        out_shape=jax.ShapeDtypeStruct((B, S, D), F32),
        compiler_params=_params(("arbitrary", "arbitrary")),
        name="combine",
    )(cnt, ye, scol, x13, g_final.reshape(1, D))


def kernel(x, g_mix, w_in, conv_a_w, conv_a_b, lru_w_r, lru_b_r, lru_w_i, lru_b_i, lru_lambda, w_a_out, conv_b_w, conv_b_b, filt_w1, filt_b1, filt_w2, filt_b2, filt_w3, filt_freq, filt_bias, w_b_out, w_o, g_ffn, w_router, w_gate, w_up, w_down, g_final):
    B, S, D = x.shape
    assert w_in.shape[0] == 1, "single-layer block only"
    l = 0
    lru_w = conv_a_w.shape[2]
    hy_w = filt_bias.shape[2]
    n_exp = w_router.shape[2]
    cap = CAPACITY_FACTOR * S // n_exp
    assert 2 * S == N_FFT
    ma32, ma64, gb, gbi, mai = (jnp.asarray(c).astype(BF16) for c in _dft_constants())

    x2 = x.reshape(B * S, D)
    proj2 = _inproj(x2, g_mix[l], w_in[l])
    proj3 = proj2.reshape(B, S, -1)
    n_ff = w_gate.shape[3]
    ua, wu = _rglru(proj3, conv_a_w[l], conv_a_b[l], lru_w_r[l], lru_b_r[l], lru_w_i[l], lru_b_i[l], lru_lambda[l],
                    w_up[l].reshape(n_exp * D, n_ff))
    kf = _filter_spectra(S, filt_w1[l], filt_b1[l], filt_w2[l], filt_b2[l], filt_w3[l], filt_freq[l], hy_w, ma64, gb)
    zb = _hyena(proj3, 2 * lru_w, conv_b_w[l], conv_b_b[l], filt_bias[l], kf, ma32, gb, gbi, mai)
    x1, h, logits, wd = _merge(ua.reshape(B * S, lru_w), zb.reshape(B * S, hy_w), proj2, 2 * lru_w + 3 * hy_w, x2,
                               w_a_out[l].astype(BF16), w_b_out[l].astype(BF16), w_o[l].astype(BF16), g_ffn[l],
                               w_router[l], w_down[l].reshape(n_exp * n_ff, D))
    affrow, srow, scol, cnt = _route(logits.reshape(B, S, -1), n_exp, cap)
    cnt = cnt[:, :, :S // TOKEN_BLOCK + 1].reshape(-1)
    xe, vals, wg = _gather(h.reshape(B, S, D), srow, affrow, cnt, cap, w_gate[l].reshape(n_exp * D, n_ff))
    ye = _ffn(xe, vals, wg.reshape(n_exp, D, n_ff), wu.reshape(n_exp, D, n_ff), wd.reshape(n_exp, n_ff, D))
    return _combine(ye, scol, cnt, x1.reshape(B, S, D), g_final)
```
